```python
import math
import jax, jax.numpy as jnp
from jax import lax
import numpy as np

D_MODEL = 1024
BATCH = 16
SEQ = 4096
DEPTH = 1

DIFF_HEADS = 4
DIFF_HEAD_DIM = 64
DIFF_V_DIM = 2 * DIFF_HEAD_DIM
DIFF_ROT = DIFF_HEAD_DIM // 4
MLA_HEADS = 8
MLA_NOPE = 64
MLA_ROPE = 32
MLA_V = 64
MLA_Q_RANK = 384
MLA_KV_RANK = 256
D_FF = 4 * D_MODEL
N_BRANCHES = 2
ROPE_THETA = 500000.0
Q_BLOCK = 128
LN_EPS = 1e-5
RMS_EPS = 1e-6
MASK_VALUE = -1e30
ALPHA = (2.0 * DEPTH) ** 0.25
BETA = (8.0 * DEPTH) ** -0.25

DIFF_Q_COLS = DIFF_HEADS * 2 * DIFF_HEAD_DIM
DIFF_K_COLS = DIFF_HEADS * 2 * DIFF_HEAD_DIM
DIFF_V_COLS = DIFF_HEADS * DIFF_V_DIM
GATE_COLS = N_BRANCHES * D_MODEL
SPLIT_SIZES = (DIFF_Q_COLS, DIFF_K_COLS, DIFF_V_COLS, MLA_Q_RANK, MLA_KV_RANK, MLA_ROPE, GATE_COLS)
SPLIT_POINTS = tuple(int(s) for s in np.cumsum(SPLIT_SIZES)[:-1])
D_IN = int(sum(SPLIT_SIZES))
DIFF_OUT = DIFF_HEADS * DIFF_V_DIM
MLA_OUT = MLA_HEADS * MLA_V

kernel_name = "hybrid_diffattn_mla_gated_deepnorm"


def layer_norm(x, g, b):
    xf = x.astype(jnp.float32)
    mu = jnp.mean(xf, axis=-1, keepdims=True)
    var = jnp.mean(jnp.square(xf - mu), axis=-1, keepdims=True)
    return ((xf - mu) * lax.rsqrt(var + LN_EPS) * g.astype(jnp.float32) + b.astype(jnp.float32)).astype(x.dtype)


def rms_norm(x, g):
    xf = x.astype(jnp.float32)
    ms = jnp.mean(jnp.square(xf), axis=-1, keepdims=True)
    return (xf * lax.rsqrt(ms + RMS_EPS) * g.astype(jnp.float32)).astype(x.dtype)


def apply_rope(x, positions, rot_dim):
    half = rot_dim // 2
    inv_freq = jnp.power(ROPE_THETA, -jnp.arange(half, dtype=jnp.float32) / half)
    ang = positions.astype(jnp.float32)[:, :, None] * inv_freq
    ang = ang.reshape(ang.shape[:2] + (1,) * (x.ndim - 3) + (half,))
    cos = jnp.cos(ang).astype(x.dtype)
    sin = jnp.sin(ang).astype(x.dtype)
    x1 = x[..., :half]
    x2 = x[..., half:rot_dim]
    return jnp.concatenate([x1 * cos - x2 * sin, x2 * cos + x1 * sin, x[..., rot_dim:]], axis=-1)


def multi_map_causal_attention(q, k, v, map_w, scale):
    B, S, M, H, D = q.shape
    Dv = v.shape[-1]
    nb = S // Q_BLOCK
    qb = jnp.moveaxis(q.reshape(B, nb, Q_BLOCK, M, H, D), 1, 0)
    k_pos = jnp.arange(S)

    def one_block(args):
        i, q_i = args
        s = jnp.einsum('bqmhd,bkmhd->bmhqk', q_i, k).astype(jnp.float32) * scale
        q_pos = i * Q_BLOCK + jnp.arange(Q_BLOCK)
        mask = k_pos[None, :] <= q_pos[:, None]
        p = jax.nn.softmax(jnp.where(mask, s, MASK_VALUE), axis=-1)
        p = jnp.einsum('mh,bmhqk->bhqk', map_w.astype(jnp.float32), p)
        return jnp.einsum('bhqk,bkhd->bqhd', p.astype(v.dtype), v)

    out = lax.map(one_block, (jnp.arange(nb), qb))
    return jnp.moveaxis(out, 0, 1).reshape(B, S, H, Dv)


def token_mixer(x, positions, w_in, gate_b, diff_lambda, diff_subln_g, mla_q_norm_g, w_uq,
                mla_kv_norm_g, w_ukv, w_o_diff, w_o_mla, w_out, lambda_init):
    B, S, _ = x.shape
    z = jnp.einsum('bsd,de->bse', x, w_in)
    dq, dk, dv, cq, ckv, kr, gates = jnp.split(z, SPLIT_POINTS, axis=-1)

    dq = apply_rope(dq.reshape(B, S, DIFF_HEADS, 2, DIFF_HEAD_DIM), positions, DIFF_ROT).swapaxes(2, 3)
    dk = apply_rope(dk.reshape(B, S, DIFF_HEADS, 2, DIFF_HEAD_DIM), positions, DIFF_ROT).swapaxes(2, 3)
    dv = dv.reshape(B, S, DIFF_HEADS, DIFF_V_DIM)
    lam = diff_lambda.astype(jnp.float32)
    lam_full = jnp.exp(jnp.sum(lam[0] * lam[1])) - jnp.exp(jnp.sum(lam[2] * lam[3])) + lambda_init
    ones_h = jnp.ones((DIFF_HEADS,), jnp.float32)
    map_w = jnp.stack([ones_h, -lam_full * ones_h])
    o_diff = multi_map_causal_attention(dq, dk, dv, map_w, DIFF_HEAD_DIM ** -0.5)
    o_diff = rms_norm(o_diff, diff_subln_g) * (1.0 - lambda_init)
    o_diff = o_diff.reshape(B, S, DIFF_OUT)

    q = jnp.einsum('bsr,re->bse', rms_norm(cq, mla_q_norm_g), w_uq).reshape(B, S, MLA_HEADS, MLA_NOPE + MLA_ROPE)
    q = jnp.concatenate([q[..., :MLA_NOPE], apply_rope(q[..., MLA_NOPE:], positions, MLA_ROPE)], axis=-1)
    kv = jnp.einsum('bsr,re->bse', rms_norm(ckv, mla_kv_norm_g), w_ukv).reshape(B, S, MLA_HEADS, MLA_NOPE + MLA_V)
    k_nope, v = kv[..., :MLA_NOPE], kv[..., MLA_NOPE:]
    k_rope = apply_rope(kr[:, :, None, :], positions, MLA_ROPE)
    k = jnp.concatenate([k_nope, jnp.broadcast_to(k_rope, (B, S, MLA_HEADS, MLA_ROPE))], axis=-1)
    o_mla = multi_map_causal_attention(q[:, :, None], k[:, :, None], v,
                                       jnp.ones((1, MLA_HEADS), jnp.float32),
                                       (MLA_NOPE + MLA_ROPE) ** -0.5)
    o_mla = o_mla.reshape(B, S, MLA_OUT)

    g = jax.nn.sigmoid(gates.reshape(B, S, N_BRANCHES, D_MODEL) + gate_b)
    y = g[:, :, 0] * jnp.einsum('bse,ed->bsd', o_diff, w_o_diff) \
        + g[:, :, 1] * jnp.einsum('bse,ed->bsd', o_mla, w_o_mla)
    return jnp.einsum('bsd,de->bse', y, w_out)


def squared_relu_mlp(x, w_up, w_down):
    h = jnp.square(jax.nn.relu(jnp.einsum('bsd,df->bsf', x, w_up)))
    return jnp.einsum('bsf,fd->bsd', h, w_down)


def setup_inputs(seed: int = 0) -> dict:
    key = jax.random.key(seed)
    ks = jax.random.split(key, 20)
    f32 = jnp.float32

    def nrm(k, shape, scale):
        return jax.random.normal(k, shape, f32) * scale

    def gain(k, shape):
        return 1.0 + 0.02 * jax.random.normal(k, shape, f32)

    x = jax.random.normal(ks[0], (BATCH, SEQ, D_MODEL), f32)
    offsets = jax.random.randint(ks[1], (BATCH, 1), 0, 1024, dtype=jnp.int32)
    positions = offsets + jnp.arange(SEQ, dtype=jnp.int32)[None, :]
    return {
        "x": x,
        "positions": positions,
        "w_in": nrm(ks[2], (DEPTH, D_MODEL, D_IN), D_MODEL ** -0.5),
        "gate_b": nrm(ks[3], (DEPTH, N_BRANCHES, D_MODEL), 0.02),
        "diff_lambda": nrm(ks[4], (DEPTH, 4, DIFF_HEAD_DIM), 0.1),
        "diff_subln_g": gain(ks[5], (DEPTH, DIFF_V_DIM)),
        "mla_q_norm_g": gain(ks[6], (DEPTH, MLA_Q_RANK)),
        "w_uq": nrm(ks[7], (DEPTH, MLA_Q_RANK, MLA_HEADS * (MLA_NOPE + MLA_ROPE)), MLA_Q_RANK ** -0.5),
        "mla_kv_norm_g": gain(ks[8], (DEPTH, MLA_KV_RANK)),
        "w_ukv": nrm(ks[9], (DEPTH, MLA_KV_RANK, MLA_HEADS * (MLA_NOPE + MLA_V)), MLA_KV_RANK ** -0.5),
        "w_o_diff": nrm(ks[10], (DEPTH, DIFF_OUT, D_MODEL), BETA * DIFF_OUT ** -0.5),
        "w_o_mla": nrm(ks[11], (DEPTH, MLA_OUT, D_MODEL), BETA * MLA_OUT ** -0.5),
        "w_out": nrm(ks[12], (DEPTH, D_MODEL, D_MODEL), BETA * D_MODEL ** -0.5),
        "ln1_g": gain(ks[13], (DEPTH, D_MODEL)),
        "ln1_b": nrm(ks[14], (DEPTH, D_MODEL), 0.02),
        "w_up": nrm(ks[15], (DEPTH, D_MODEL, D_FF), D_MODEL ** -0.5),
        "w_down": nrm(ks[16], (DEPTH, D_FF, D_MODEL), BETA * D_FF ** -0.5),
        "ln2_g": gain(ks[17], (DEPTH, D_MODEL)),
        "ln2_b": nrm(ks[18], (DEPTH, D_MODEL), 0.02),
    }


def reference(x, positions, w_in, gate_b, diff_lambda, diff_subln_g, mla_q_norm_g, w_uq,
              mla_kv_norm_g, w_ukv, w_o_diff, w_o_mla, w_out, ln1_g, ln1_b, w_up, w_down,
              ln2_g, ln2_b):
    for l in range(DEPTH):
        lambda_init = 0.8 - 0.6 * math.exp(-0.3 * l)
        h = token_mixer(x, positions, w_in[l], gate_b[l], diff_lambda[l], diff_subln_g[l],
                        mla_q_norm_g[l], w_uq[l], mla_kv_norm_g[l], w_ukv[l],
                        w_o_diff[l], w_o_mla[l], w_out[l], lambda_init)
        x = layer_norm(ALPHA * x + h, ln1_g[l], ln1_b[l])
        x = layer_norm(ALPHA * x + squared_relu_mlp(x, w_up[l], w_down[l]), ln2_g[l], ln2_b[l])
    return x
```

```python
import functools
import math

import jax
import jax.numpy as jnp
import numpy as np
from jax import lax
from jax.experimental import pallas as pl
from jax.experimental.pallas import tpu as pltpu

F32 = jnp.float32
BF16 = jnp.bfloat16

D_MODEL = 1024
DIFF_HEADS = 4
DIFF_HEAD_DIM = 64
DIFF_V_DIM = 128
DIFF_ROT = 16
MLA_HEADS = 8
MLA_NOPE = 64
MLA_ROPE = 32
MLA_V = 64
MLA_Q_RANK = 384
MLA_KV_RANK = 256
D_FF = 4096
ROPE_THETA = 500000.0
LN_EPS = 1e-5
RMS_EPS = 1e-6
MASK_VALUE = -1e30

LANES = 128
TOK = 512
VMEM_LIMIT = 56 * 1024 * 1024

LOG2E = math.log2(math.e)
Q_SCALE_DIFF = (DIFF_HEAD_DIM ** -0.5) * LOG2E
Q_SCALE_MLA = ((MLA_NOPE + MLA_ROPE) ** -0.5) * LOG2E

_NT = (((1,), (1,)), ((), ()))
_TN = (((0,), (0,)), ((), ()))


def _const_spec(shape):
    n = len(shape)
    return pl.BlockSpec(shape, lambda *_: (0,) * n, pipeline_mode=pl.Buffered(1))


def _rope_tables():
    lane = np.arange(LANES)
    tab = np.zeros((8, LANES), np.float32)
    half = DIFF_ROT // 2
    inv = jnp.power(ROPE_THETA, -jnp.arange(half, dtype=F32) / half)
    j = lane % DIFF_HEAD_DIM
    rot = j < DIFF_ROT
    row0 = jnp.where(rot, inv[j % half], 0.0)
    tab[1] = np.where(j < half, -1.0, 0.0)
    tab[2] = np.where((j >= half) & rot, 1.0, 0.0)
    half_m = MLA_ROPE // 2
    inv_m = jnp.power(ROPE_THETA, -jnp.arange(half_m, dtype=F32) / half_m)
    jm = lane - MLA_NOPE
    rot_m = (jm >= 0) & (jm < MLA_ROPE)
    row3 = jnp.where(rot_m, inv_m[np.clip(jm, 0, MLA_ROPE - 1) % half_m], 0.0)
    tab[4] = np.where(rot_m & (jm < half_m), -1.0, 0.0)
    tab[5] = np.where(rot_m & (jm >= half_m), 1.0, 0.0)
    t = jnp.asarray(tab)
    return t.at[0].set(row0).at[3].set(row3)


def _rope_slab(z, cos, sin_lo, sin_hi, half):
    up = pltpu.roll(z, LANES - half, 1)
    dn = pltpu.roll(z, half, 1)
    return z * cos + up * sin_lo + dn * sin_hi


def _in_proj_kernel(x_ref, pos_ref, tab_ref, wqk_ref, wvt_ref, wc_ref, wg_ref, wuq_ref, wuk_ref, wuvt_ref,
                    gb_ref, qng_ref, kvng_ref,
                    dq0_ref, dq1_ref, dk_ref, dvt_ref, qm_ref, km_ref, vmt_ref, g_ref):
    xb = x_ref[...].astype(BF16)
    pos = pos_ref[...].astype(F32)
    tab = tab_ref[...]

    ang = pos * tab[0:1, :]
    cos_d, sin_d = jnp.cos(ang), jnp.sin(ang)
    lo_d, hi_d = sin_d * tab[1:2, :], sin_d * tab[2:3, :]
    ang = pos * tab[3:4, :]
    cos_m, sin_m = jnp.cos(ang), jnp.sin(ang)
    lo_m, hi_m = sin_m * tab[4:5, :], sin_m * tab[5:6, :]

    first_half = lax.broadcasted_iota(jnp.int32, (TOK, LANES), 1) < DIFF_HEAD_DIM

    zq = jnp.dot(xb, wqk_ref[:, 0:512], preferred_element_type=F32)
    for h in range(DIFF_HEADS):
        sl = slice(h * LANES, (h + 1) * LANES)
        r = _rope_slab(zq[:, sl], cos_d, lo_d, hi_d, DIFF_ROT // 2) * Q_SCALE_DIFF
        dq0_ref[:, sl] = jnp.where(first_half, r, 0.0).astype(BF16)
        dq1_ref[:, sl] = jnp.where(first_half, 0.0, r).astype(BF16)
    zk = jnp.dot(xb, wqk_ref[:, 512:1024], preferred_element_type=F32)
    for h in range(DIFF_HEADS):
        sl = slice(h * LANES, (h + 1) * LANES)
        dk_ref[:, sl] = _rope_slab(zk[:, sl], cos_d, lo_d, hi_d, DIFF_ROT // 2).astype(BF16)
    dvt_ref[0, 0] = lax.dot_general(wvt_ref[...], xb, _NT, preferred_element_type=F32).astype(BF16)

    zc = jnp.dot(xb, wc_ref[...], preferred_element_type=F32)
    cq = zc[:, 0:MLA_Q_RANK]
    ckv = zc[:, MLA_Q_RANK:MLA_Q_RANK + MLA_KV_RANK]
    kr = zc[:, MLA_Q_RANK + MLA_KV_RANK:]
    cqn = cq * lax.rsqrt(jnp.mean(cq * cq, axis=-1, keepdims=True) + RMS_EPS) * qng_ref[...]
    ckvn = (ckv * lax.rsqrt(jnp.mean(ckv * ckv, axis=-1, keepdims=True) + RMS_EPS) * kvng_ref[...]).astype(BF16)
    q = jnp.dot(cqn.astype(BF16), wuq_ref[...], preferred_element_type=F32)
    kn = jnp.dot(ckvn, wuk_ref[...], preferred_element_type=F32)
    kr_rot = _rope_slab(kr, cos_m, lo_m, hi_m, MLA_ROPE // 2)
    for h in range(MLA_HEADS):
        sl = slice(h * LANES, (h + 1) * LANES)
        qm_ref[:, sl] = (_rope_slab(q[:, sl], cos_m, lo_m, hi_m, MLA_ROPE // 2) * Q_SCALE_MLA).astype(BF16)
        km_ref[:, sl] = (kn[:, sl] + kr_rot).astype(BF16)
    vmt_ref[0, 0] = lax.dot_general(wuvt_ref[...], ckvn, _NT, preferred_element_type=F32).astype(BF16)

    for c in range(4):
        sl = slice(c * 512, (c + 1) * 512)
        zg = jnp.dot(xb, wg_ref[:, sl], preferred_element_type=F32) + gb_ref[:, sl]
        g_ref[:, sl] = jax.nn.sigmoid(zg).astype(BF16)


def _attn_kernel(*refs, units, diff, lambda_init):
    if diff:
        lam_ref, gcol_ref, q0_ref, q1_ref, k_ref, vt_ref, ot_ref, m_sc, l_sc, acc_sc = refs
        q_refs = (q0_ref, q1_ref)
    else:
        q0_ref, k_ref, vt_ref, ot_ref, m_sc, l_sc, acc_sc = refs
        q_refs = (q0_ref,)
    n_blk = vt_ref.shape[1]

    def step(qi, kj, masked):
        for u, (qr, col, vrow, vrows) in enumerate(units):
            q = q_refs[qr][pl.ds(qi * TOK, TOK), col:col + LANES]
            k = k_ref[pl.ds(kj * TOK, TOK), col:col + LANES]
            s = lax.dot_general(k, q, _NT, preferred_element_type=F32)
            if masked:
                key = lax.broadcasted_iota(jnp.int32, (TOK, TOK), 0)
                qry = lax.broadcasted_iota(jnp.int32, (TOK, TOK), 1)
                s = jnp.where(key <= qry, s, MASK_VALUE)
            m_old = m_sc[u]
            m_new = jnp.maximum(m_old, jnp.max(s, axis=0, keepdims=True))
            alpha = jnp.exp2(m_old - m_new)
            p = jnp.exp2(s - m_new)
            l_sc[u] = alpha * l_sc[u] + jnp.sum(p, axis=0, keepdims=True)
            m_sc[u] = m_new
            v = vt_ref[0, kj, vrow:vrow + vrows, :]
            pv = jnp.dot(v, p.astype(BF16), preferred_element_type=F32)
            acc_sc[u, 0:vrows, :] = alpha * acc_sc[u, 0:vrows, :] + pv

    def q_block(qi, carry):
        m_sc[...] = jnp.full(m_sc.shape, MASK_VALUE, F32)
        l_sc[...] = jnp.zeros(l_sc.shape, F32)
        acc_sc[...] = jnp.zeros(acc_sc.shape, F32)

        def k_block(kj, c):
            step(qi, kj, False)
            return c

        lax.fori_loop(0, qi, k_block, 0)
        step(qi, qi, True)

        if diff:
            lam = lam_ref[0]
            for hh in range(len(units) // 2):
                u0, u1 = 2 * hh, 2 * hh + 1
                vrow, vrows = units[u0][2], units[u0][3]
                o = acc_sc[u0] * (1.0 / l_sc[u0]) - lam * (acc_sc[u1] * (1.0 / l_sc[u1]))
                ms = jnp.mean(o * o, axis=0, keepdims=True)
                o = o * lax.rsqrt(ms + RMS_EPS) * gcol_ref[...] * (1.0 - lambda_init)
                ot_ref[0, qi, vrow:vrow + vrows, :] = o.astype(BF16)
        else:
            for u, (_, _, vrow, vrows) in enumerate(units):
                o = acc_sc[u, 0:vrows, :] * (1.0 / l_sc[u])
                ot_ref[0, qi, vrow:vrow + vrows, :] = o.astype(BF16)
        return carry

    lax.fori_loop(0, n_blk, q_block, 0)


def _layer_norm(r, g, b):
    mu = jnp.mean(r, axis=-1, keepdims=True)
    d = r - mu
    var = jnp.mean(d * d, axis=-1, keepdims=True)
    return d * lax.rsqrt(var + LN_EPS) * g + b


def _out_mlp_kernel(x_ref, od_ref, om_ref, g_ref, wod_ref, wom_ref, wout_ref, l1g_ref, l1b_ref,
                    wup_ref, wdn_ref, l2g_ref, l2b_ref, o_ref, *, alpha):
    yd = lax.dot_general(od_ref[0, 0], wod_ref[...], _TN, preferred_element_type=F32)
    ym = lax.dot_general(om_ref[0, 0], wom_ref[...], _TN, preferred_element_type=F32)
    y = g_ref[:, 0:D_MODEL].astype(F32) * yd + g_ref[:, D_MODEL:].astype(F32) * ym
    h = jnp.dot(y.astype(BF16), wout_ref[...], preferred_element_type=F32)
    x1 = _layer_norm(alpha * x_ref[...] + h, l1g_ref[...], l1b_ref[...])
    u = jnp.dot(x1.astype(BF16), wup_ref[...], preferred_element_type=F32)
    u = jnp.maximum(u, 0.0)
    u = (u * u).astype(BF16)
    f = jnp.dot(u, wdn_ref[...], preferred_element_type=F32)
    o_ref[...] = _layer_norm(alpha * x1 + f, l2g_ref[...], l2b_ref[...])


def _layer(x2, pos2, tab, w_in, gate_b, diff_lambda, diff_subln_g, mla_q_norm_g, w_uq, mla_kv_norm_g, w_ukv,
           w_o_diff, w_o_mla, w_out, ln1_g, ln1_b, w_up, w_down, ln2_g, ln2_b, *, batch, seq, lambda_init, alpha):
    n_tok = batch * seq
    n_blk = seq // TOK
    n_tiles = n_tok // TOK

    c0, c1, c2, c3, c4, c5 = 512, 1024, 1536, 1920, 2176, 2208
    wqk = w_in[:, 0:c1].astype(BF16)
    wvt = w_in[:, c1:c2].T.astype(BF16)
    kr_pad = jnp.zeros((D_MODEL, LANES), F32).at[:, MLA_NOPE:MLA_NOPE + MLA_ROPE].set(w_in[:, c4:c5])
    wc = jnp.concatenate([w_in[:, c2:c4], kr_pad], axis=1).astype(BF16)
    wg = w_in[:, c5:].astype(BF16)
    wuq3 = w_uq.reshape(MLA_Q_RANK, MLA_HEADS, MLA_NOPE + MLA_ROPE)
    wuq = jnp.pad(wuq3, ((0, 0), (0, 0), (0, LANES - MLA_NOPE - MLA_ROPE))).reshape(MLA_Q_RANK, MLA_HEADS * LANES)
    wukv3 = w_ukv.reshape(MLA_KV_RANK, MLA_HEADS, MLA_NOPE + MLA_V)
    wuk = jnp.pad(wukv3[:, :, :MLA_NOPE], ((0, 0), (0, 0), (0, LANES - MLA_NOPE))).reshape(MLA_KV_RANK, MLA_HEADS * LANES)
    wuvt = wukv3[:, :, MLA_NOPE:].reshape(MLA_KV_RANK, MLA_HEADS * MLA_V).T
    wuq, wuk, wuvt = wuq.astype(BF16), wuk.astype(BF16), wuvt.astype(BF16)
    gb = gate_b.reshape(1, 2 * D_MODEL)
    lam = diff_lambda.astype(F32)
    lam_full = (jnp.exp(jnp.sum(lam[0] * lam[1])) - jnp.exp(jnp.sum(lam[2] * lam[3])) + lambda_init).reshape(1)

    tok_spec = lambda w: pl.BlockSpec((TOK, w), lambda i: (i, 0))
    feat_spec = pl.BlockSpec((1, 1, 512, TOK), lambda i: (i // n_blk, i % n_blk, 0, 0))
    params1 = pltpu.CompilerParams(dimension_semantics=("arbitrary",), vmem_limit_bytes=VMEM_LIMIT)

    dq0, dq1, dk, dvt, qm, km, vmt, g = pl.pallas_call(
        _in_proj_kernel,
        grid=(n_tiles,),
        in_specs=[tok_spec(D_MODEL), tok_spec(1), _const_spec((8, LANES)),
                  _const_spec(wqk.shape), _const_spec(wvt.shape), _const_spec(wc.shape), _const_spec(wg.shape),
                  _const_spec(wuq.shape), _const_spec(wuk.shape), _const_spec(wuvt.shape),
                  _const_spec(gb.shape), _const_spec((1, MLA_Q_RANK)), _const_spec((1, MLA_KV_RANK))],
        out_specs=[tok_spec(512), tok_spec(512), tok_spec(512), feat_spec,
                   tok_spec(1024), tok_spec(1024), feat_spec, tok_spec(2048)],
        out_shape=[jax.ShapeDtypeStruct((n_tok, 512), BF16), jax.ShapeDtypeStruct((n_tok, 512), BF16),
                   jax.ShapeDtypeStruct((n_tok, 512), BF16), jax.ShapeDtypeStruct((batch, n_blk, 512, TOK), BF16),
                   jax.ShapeDtypeStruct((n_tok, 1024), BF16), jax.ShapeDtypeStruct((n_tok, 1024), BF16),
                   jax.ShapeDtypeStruct((batch, n_blk, 512, TOK), BF16), jax.ShapeDtypeStruct((n_tok, 2048), BF16)],
        compiler_params=params1,
        name="in_proj",
    )(x2, pos2, tab, wqk, wvt, wc, wg, wuq, wuk, wuvt, gb,
      mla_q_norm_g.reshape(1, MLA_Q_RANK), mla_kv_norm_g.reshape(1, MLA_KV_RANK))

    params2 = pltpu.CompilerParams(dimension_semantics=("arbitrary", "arbitrary"), vmem_limit_bytes=VMEM_LIMIT)
    feat_blk = pl.BlockSpec((1, n_blk, 256, TOK), lambda b, gI: (b, 0, gI, 0))
    ot_shape = jax.ShapeDtypeStruct((batch, n_blk, 512, TOK), BF16)

    seq256 = pl.BlockSpec((seq, 256), lambda b, gI: (b, gI))
    diff_units = ((0, 0, 0, 128), (1, 0, 0, 128), (0, 128, 128, 128), (1, 128, 128, 128))
    odt = pl.pallas_call(
        functools.partial(_attn_kernel, units=diff_units, diff=True, lambda_init=lambda_init),
        grid=(batch, 2),
        in_specs=[pl.BlockSpec(memory_space=pltpu.SMEM), _const_spec((DIFF_V_DIM, 1)),
                  seq256, seq256, seq256, feat_blk],
        out_specs=feat_blk,
        out_shape=ot_shape,
        scratch_shapes=[pltpu.VMEM((4, 1, TOK), F32), pltpu.VMEM((4, 1, TOK), F32),
                        pltpu.VMEM((4, DIFF_V_DIM, TOK), F32)],
        compiler_params=params2,
        name="diff_attn",
    )(lam_full, diff_subln_g.reshape(DIFF_V_DIM, 1), dq0, dq1, dk, dvt)

    seq512 = pl.BlockSpec((seq, 512), lambda b, gI: (b, gI))
    mla_units = tuple((0, u * LANES, u * MLA_V, MLA_V) for u in range(4))
    omt = pl.pallas_call(
        functools.partial(_attn_kernel, units=mla_units, diff=False, lambda_init=lambda_init),
        grid=(batch, 2),
        in_specs=[seq512, seq512, feat_blk],
        out_specs=feat_blk,
        out_shape=ot_shape,
        scratch_shapes=[pltpu.VMEM((4, 1, TOK), F32), pltpu.VMEM((4, 1, TOK), F32),
                        pltpu.VMEM((4, MLA_V, TOK), F32)],
        compiler_params=params2,
        name="mla_attn",
    )(qm, km, vmt)

    wod, wom, wo = w_o_diff.astype(BF16), w_o_mla.astype(BF16), w_out.astype(BF16)
    wup, wdn = w_up.astype(BF16), w_down.astype(BF16)
    row = lambda v: v.reshape(1, D_MODEL)
    out = pl.pallas_call(
        functools.partial(_out_mlp_kernel, alpha=alpha),
        grid=(n_tiles,),
        in_specs=[tok_spec(D_MODEL), feat_spec, feat_spec, tok_spec(2048),
                  _const_spec(wod.shape), _const_spec(wom.shape), _const_spec(wo.shape),
                  _const_spec((1, D_MODEL)), _const_spec((1, D_MODEL)),
                  _const_spec(wup.shape), _const_spec(wdn.shape),
                  _const_spec((1, D_MODEL)), _const_spec((1, D_MODEL))],
        out_specs=tok_spec(D_MODEL),
        out_shape=jax.ShapeDtypeStruct((n_tok, D_MODEL), F32),
        compiler_params=params1,
        name="out_mlp",
    )(x2, odt, omt, g, wod, wom, wo, row(ln1_g), row(ln1_b), wup, wdn, row(ln2_g), row(ln2_b))
    return out


def kernel(x, positions, w_in, gate_b, diff_lambda, diff_subln_g, mla_q_norm_g, w_uq, mla_kv_norm_g, w_ukv,
           w_o_diff, w_o_mla, w_out, ln1_g, ln1_b, w_up, w_down, ln2_g, ln2_b):
    batch, seq, d = x.shape
    depth = w_in.shape[0]
    alpha = (2.0 * depth) ** 0.25
    tab = _rope_tables()
    x2 = x.reshape(batch * seq, d)
    pos2 = positions.reshape(batch * seq, 1)
    for l in range(depth):
        lambda_init = 0.8 - 0.6 * math.exp(-0.3 * l)
        x2 = _layer(x2, pos2, tab, w_in[l], gate_b[l], diff_lambda[l], diff_subln_g[l], mla_q_norm_g[l], w_uq[l],
                    mla_kv_norm_g[l], w_ukv[l], w_o_diff[l], w_o_mla[l], w_out[l], ln1_g[l], ln1_b[l],
                    w_up[l], w_down[l], ln2_g[l], ln2_b[l],
                    batch=batch, seq=seq, lambda_init=lambda_init, alpha=alpha)
    return x2.reshape(batch, seq, d)
```

```python
import functools
import math

import jax
import jax.numpy as jnp
import numpy as np
from jax import lax
from jax.experimental import pallas as pl
from jax.experimental.pallas import tpu as pltpu

F32 = jnp.float32
BF16 = jnp.bfloat16

D_MODEL = 1024
DIFF_HEADS = 4
DIFF_HEAD_DIM = 64
DIFF_V_DIM = 128
DIFF_ROT = 16
MLA_HEADS = 8
MLA_NOPE = 64
MLA_ROPE = 32
MLA_V = 64
MLA_Q_RANK = 384
MLA_KV_RANK = 256
D_FF = 4096
ROPE_THETA = 500000.0
LN_EPS = 1e-5
RMS_EPS = 1e-6
MASK_VALUE = -1e30

LANES = 128
TOK = 512
VMEM_LIMIT = 56 * 1024 * 1024

LOG2E = math.log2(math.e)
Q_SCALE_DIFF = (DIFF_HEAD_DIM ** -0.5) * LOG2E
Q_SCALE_MLA = ((MLA_NOPE + MLA_ROPE) ** -0.5) * LOG2E

_NT = (((1,), (1,)), ((), ()))
_TN = (((0,), (0,)), ((), ()))


def _const_spec(shape):
    n = len(shape)
    return pl.BlockSpec(shape, lambda *_: (0,) * n, pipeline_mode=pl.Buffered(1))


def _rope_freqs():
    half_d, half_m = DIFF_ROT // 2, MLA_ROPE // 2
    inv_d = jnp.power(ROPE_THETA, -jnp.arange(half_d, dtype=F32) / half_d)
    inv_m = jnp.power(ROPE_THETA, -jnp.arange(half_m, dtype=F32) / half_m)
    return jnp.concatenate([inv_d, inv_m, jnp.zeros((32 - half_d - half_m,), F32)]).reshape(32, 1)


def _rope_lane_tables(pos_row, freq_col):
    n = pos_row.shape[1]
    ang = freq_col * pos_row
    c, s = jnp.cos(ang), jnp.sin(ang)
    cd, sd, cm, sm = c[0:8], s[0:8], c[8:24], s[8:24]
    ones = lambda r: jnp.ones((r, n), F32)
    zeros = lambda r: jnp.zeros((r, n), F32)
    rows = lambda *parts: jnp.concatenate(parts, axis=0).T
    cos_d = rows(cd, cd, ones(48), cd, cd, ones(48))
    lo_d = rows(-sd, zeros(56), -sd, zeros(56))
    hi_d = rows(zeros(8), sd, zeros(56), sd, zeros(48))
    cos_m = rows(ones(64), cm, cm, ones(32))
    lo_m = rows(zeros(64), -sm, zeros(48))
    hi_m = rows(zeros(80), sm, zeros(32))
    return (cos_d, lo_d, hi_d), (cos_m, lo_m, hi_m)


def _rope_slab(z, cos, sin_lo, sin_hi, half):
    up = pltpu.roll(z, LANES - half, 1)
    dn = pltpu.roll(z, half, 1)
    return z * cos + up * sin_lo + dn * sin_hi


def _in_proj_kernel(x_ref, pos_ref, tab_ref, wqk_ref, wvt_ref, wc_ref, wg_ref, wuq_ref, wuk_ref, wuvt_ref,
                    gb_ref, qng_ref, kvng_ref,
                    dq0_ref, dq1_ref, dk_ref, dvt_ref, qm_ref, km_ref, vmt_ref, g_ref):
    xb = x_ref[...].astype(BF16)
    pos = pos_ref[0].astype(F32)
    (cos_d, lo_d, hi_d), (cos_m, lo_m, hi_m) = _rope_lane_tables(pos, tab_ref[...])

    first_half = lax.broadcasted_iota(jnp.int32, (TOK, LANES), 1) < DIFF_HEAD_DIM

    zq = jnp.dot(xb, wqk_ref[:, 0:512], preferred_element_type=F32)
    for h in range(DIFF_HEADS):
        sl = slice(h * LANES, (h + 1) * LANES)
        r = _rope_slab(zq[:, sl], cos_d, lo_d, hi_d, DIFF_ROT // 2) * Q_SCALE_DIFF
        dq0_ref[:, sl] = jnp.where(first_half, r, 0.0).astype(BF16)
        dq1_ref[:, sl] = jnp.where(first_half, 0.0, r).astype(BF16)
    zk = jnp.dot(xb, wqk_ref[:, 512:1024], preferred_element_type=F32)
    for h in range(DIFF_HEADS):
        sl = slice(h * LANES, (h + 1) * LANES)
        dk_ref[:, sl] = _rope_slab(zk[:, sl], cos_d, lo_d, hi_d, DIFF_ROT // 2).astype(BF16)
    dvt_ref[0, 0] = lax.dot_general(wvt_ref[...], xb, _NT, preferred_element_type=F32).astype(BF16)

    zc = jnp.dot(xb, wc_ref[...], preferred_element_type=F32)
    cq = zc[:, 0:MLA_Q_RANK]
    ckv = zc[:, MLA_Q_RANK:MLA_Q_RANK + MLA_KV_RANK]
    kr = zc[:, MLA_Q_RANK + MLA_KV_RANK:]
    cqn = cq * lax.rsqrt(jnp.mean(cq * cq, axis=-1, keepdims=True) + RMS_EPS) * qng_ref[...]
    ckvn = (ckv * lax.rsqrt(jnp.mean(ckv * ckv, axis=-1, keepdims=True) + RMS_EPS) * kvng_ref[...]).astype(BF16)
    q = jnp.dot(cqn.astype(BF16), wuq_ref[...], preferred_element_type=F32)
    kn = jnp.dot(ckvn, wuk_ref[...], preferred_element_type=F32)
    kr_rot = _rope_slab(kr, cos_m, lo_m, hi_m, MLA_ROPE // 2)
    for h in range(MLA_HEADS):
        sl = slice(h * LANES, (h + 1) * LANES)
        qm_ref[:, sl] = (_rope_slab(q[:, sl], cos_m, lo_m, hi_m, MLA_ROPE // 2) * Q_SCALE_MLA).astype(BF16)
        km_ref[:, sl] = (kn[:, sl] + kr_rot).astype(BF16)
    vmt_ref[0, 0] = lax.dot_general(wuvt_ref[...], ckvn, _NT, preferred_element_type=F32).astype(BF16)

    for c in range(4):
        sl = slice(c * 512, (c + 1) * 512)
        zg = jnp.dot(xb, wg_ref[:, sl], preferred_element_type=F32) + gb_ref[:, sl]
        g_ref[:, sl] = jax.nn.sigmoid(zg).astype(BF16)


def _attn_kernel(*refs, units, diff, lambda_init):
    if diff:
        lam_ref, gcol_ref, q0_ref, q1_ref, k_ref, vt_ref, ot_ref = refs[:7]
        q_refs = (q0_ref, q1_ref)
    else:
        q0_ref, k_ref, vt_ref, ot_ref = refs[:4]
        q_refs = (q0_ref,)
    m_sc, l_sc, acc_sc, s_sc, mb_sc = refs[-5:]
    n_blk = vt_ref.shape[1]
    n_units = len(units)

    def scores(u, qi, kj, slot, masked):
        qr, col, _, _ = units[u]
        q = q_refs[qr][pl.ds(qi * TOK, TOK), col:col + LANES]
        k = k_ref[pl.ds(kj * TOK, TOK), col:col + LANES]
        s = lax.dot_general(k, q, _NT, preferred_element_type=F32)
        if masked:
            key = lax.broadcasted_iota(jnp.int32, (TOK, TOK), 0)
            qry = lax.broadcasted_iota(jnp.int32, (TOK, TOK), 1)
            s = jnp.where(key <= qry, s, MASK_VALUE)
        s_sc[slot, u] = s
        mb_sc[slot, u] = jnp.max(s, axis=0, keepdims=True)

    def accumulate(u, kj, slot):
        _, _, vrow, vrows = units[u]
        m_old = m_sc[u]
        m_new = jnp.maximum(m_old, mb_sc[slot, u])
        alpha = jnp.exp2(m_old - m_new)
        p = jnp.exp2(s_sc[slot, u] - m_new)
        l_sc[u] = alpha * l_sc[u] + jnp.sum(p, axis=0, keepdims=True)
        m_sc[u] = m_new
        v = vt_ref[0, kj, vrow:vrow + vrows, :]
        pv = jnp.dot(v, p.astype(BF16), preferred_element_type=F32)
        acc_sc[u, 0:vrows, :] = alpha * acc_sc[u, 0:vrows, :] + pv

    def step(qi, kj_next, kj_cur, cur, masked):
        for u in range(n_units):
            scores(u, qi, kj_next, 1 - cur, masked)
            accumulate(u, kj_cur, cur)

    def reset():
        m_sc[...] = jnp.full(m_sc.shape, MASK_VALUE, F32)
        l_sc[...] = jnp.zeros(l_sc.shape, F32)
        acc_sc[...] = jnp.zeros(acc_sc.shape, F32)

    def finalize(qi):
        if diff:
            lam = lam_ref[0]
            for hh in range(n_units // 2):
                u0, u1 = 2 * hh, 2 * hh + 1
                vrow, vrows = units[u0][2], units[u0][3]
                o = acc_sc[u0] * (1.0 / l_sc[u0]) - lam * (acc_sc[u1] * (1.0 / l_sc[u1]))
                ms = jnp.mean(o * o, axis=0, keepdims=True)
                o = o * lax.rsqrt(ms + RMS_EPS) * gcol_ref[...] * (1.0 - lambda_init)
                ot_ref[0, qi, vrow:vrow + vrows, :] = o.astype(BF16)
        else:
            for u, (_, _, vrow, vrows) in enumerate(units):
                o = acc_sc[u, 0:vrows, :] * (1.0 / l_sc[u])
                ot_ref[0, qi, vrow:vrow + vrows, :] = o.astype(BF16)

    reset()
    for u in range(n_units):
        scores(u, 0, 0, 0, True)

    def q_block_body(qi, d):
        step(qi, qi, jnp.maximum(qi - 2, 0), 1 - d, True)
        finalize(qi - 1)
        reset()

        def k_pair(t, c):
            j = 2 * t
            step(qi, j, jnp.where(t == 0, qi, j - 1), d, False)
            step(qi, j + 1, j, 1 - d, False)
            return c

        lax.fori_loop(0, qi // 2, k_pair, 0)

        @pl.when(qi % 2 == 1)
        def _():
            step(qi, qi - 1, jnp.where(qi == 1, qi, qi - 2), d, False)

    def q_block(qi, c):
        diag_in_slot1 = ((qi + 1) // 2) % 2 == 1

        @pl.when(diag_in_slot1)
        def _():
            q_block_body(qi, 1)

        @pl.when(jnp.logical_not(diag_in_slot1))
        def _():
            q_block_body(qi, 0)

        return c

    lax.fori_loop(1, n_blk, q_block, 0)
    last_slot = 0 if (n_blk * (n_blk + 1) // 2 - 1) % 2 == 0 else 1
    for u in range(n_units):
        accumulate(u, n_blk - 2, last_slot)
    finalize(n_blk - 1)


def _layer_norm(r, g, b):
    mu = jnp.mean(r, axis=-1, keepdims=True)
    d = r - mu
    var = jnp.mean(d * d, axis=-1, keepdims=True)
    return d * lax.rsqrt(var + LN_EPS) * g + b


def _out_mlp_kernel(x_ref, od_ref, om_ref, g_ref, wod_ref, wom_ref, wout_ref, l1g_ref, l1b_ref,
                    wup_ref, wdn_ref, l2g_ref, l2b_ref, o_ref, *, alpha):
    yd = lax.dot_general(od_ref[0, 0], wod_ref[...], _TN, preferred_element_type=F32)
    ym = lax.dot_general(om_ref[0, 0], wom_ref[...], _TN, preferred_element_type=F32)
    y = g_ref[:, 0:D_MODEL].astype(F32) * yd + g_ref[:, D_MODEL:].astype(F32) * ym
    h = jnp.dot(y.astype(BF16), wout_ref[...], preferred_element_type=F32)
    x1 = _layer_norm(alpha * x_ref[...] + h, l1g_ref[...], l1b_ref[...])
    u = jnp.dot(x1.astype(BF16), wup_ref[...], preferred_element_type=F32)
    u = jnp.maximum(u, 0.0)
    u = (u * u).astype(BF16)
    f = jnp.dot(u, wdn_ref[...], preferred_element_type=F32)
    o_ref[...] = _layer_norm(alpha * x1 + f, l2g_ref[...], l2b_ref[...])


def _layer(x2, pos2, tab, w_in, gate_b, diff_lambda, diff_subln_g, mla_q_norm_g, w_uq, mla_kv_norm_g, w_ukv,
           w_o_diff, w_o_mla, w_out, ln1_g, ln1_b, w_up, w_down, ln2_g, ln2_b, *, batch, seq, lambda_init, alpha):
    n_tok = batch * seq
    n_blk = seq // TOK
    n_tiles = n_tok // TOK

    c0, c1, c2, c3, c4, c5 = 512, 1024, 1536, 1920, 2176, 2208
    wqk = w_in[:, 0:c1].astype(BF16)
    wvt = w_in[:, c1:c2].T.astype(BF16)
    kr_pad = jnp.zeros((D_MODEL, LANES), F32).at[:, MLA_NOPE:MLA_NOPE + MLA_ROPE].set(w_in[:, c4:c5])
    wc = jnp.concatenate([w_in[:, c2:c4], kr_pad], axis=1).astype(BF16)
    wg = w_in[:, c5:].astype(BF16)
    wuq3 = w_uq.reshape(MLA_Q_RANK, MLA_HEADS, MLA_NOPE + MLA_ROPE)
    wuq = jnp.pad(wuq3, ((0, 0), (0, 0), (0, LANES - MLA_NOPE - MLA_ROPE))).reshape(MLA_Q_RANK, MLA_HEADS * LANES)
    wukv3 = w_ukv.reshape(MLA_KV_RANK, MLA_HEADS, MLA_NOPE + MLA_V)
    wuk = jnp.pad(wukv3[:, :, :MLA_NOPE], ((0, 0), (0, 0), (0, LANES - MLA_NOPE))).reshape(MLA_KV_RANK, MLA_HEADS * LANES)
    wuvt = wukv3[:, :, MLA_NOPE:].reshape(MLA_KV_RANK, MLA_HEADS * MLA_V).T
    wuq, wuk, wuvt = wuq.astype(BF16), wuk.astype(BF16), wuvt.astype(BF16)
    gb = gate_b.reshape(1, 2 * D_MODEL)
    lam = diff_lambda.astype(F32)
    lam_full = (jnp.exp(jnp.sum(lam[0] * lam[1])) - jnp.exp(jnp.sum(lam[2] * lam[3])) + lambda_init).reshape(1)

    tok_spec = lambda w: pl.BlockSpec((TOK, w), lambda i: (i, 0))
    feat_spec = pl.BlockSpec((1, 1, 512, TOK), lambda i: (i // n_blk, i % n_blk, 0, 0))
    params1 = pltpu.CompilerParams(dimension_semantics=("arbitrary",), vmem_limit_bytes=VMEM_LIMIT)

    dq0, dq1, dk, dvt, qm, km, vmt, g = pl.pallas_call(
        _in_proj_kernel,
        grid=(n_tiles,),
        in_specs=[tok_spec(D_MODEL), pl.BlockSpec((1, 1, TOK), lambda i: (i, 0, 0)), _const_spec((32, 1)),
                  _const_spec(wqk.shape), _const_spec(wvt.shape), _const_spec(wc.shape), _const_spec(wg.shape),
                  _const_spec(wuq.shape), _const_spec(wuk.shape), _const_spec(wuvt.shape),
                  _const_spec(gb.shape), _const_spec((1, MLA_Q_RANK)), _const_spec((1, MLA_KV_RANK))],
        out_specs=[tok_spec(512), tok_spec(512), tok_spec(512), feat_spec,
                   tok_spec(1024), tok_spec(1024), feat_spec, tok_spec(2048)],
        out_shape=[jax.ShapeDtypeStruct((n_tok, 512), BF16), jax.ShapeDtypeStruct((n_tok, 512), BF16),
                   jax.ShapeDtypeStruct((n_tok, 512), BF16), jax.ShapeDtypeStruct((batch, n_blk, 512, TOK), BF16),
                   jax.ShapeDtypeStruct((n_tok, 1024), BF16), jax.ShapeDtypeStruct((n_tok, 1024), BF16),
                   jax.ShapeDtypeStruct((batch, n_blk, 512, TOK), BF16), jax.ShapeDtypeStruct((n_tok, 2048), BF16)],
        compiler_params=params1,
        name="in_proj",
    )(x2, pos2, tab, wqk, wvt, wc, wg, wuq, wuk, wuvt, gb,
      mla_q_norm_g.reshape(1, MLA_Q_RANK), mla_kv_norm_g.reshape(1, MLA_KV_RANK))

    params2 = pltpu.CompilerParams(dimension_semantics=("arbitrary", "arbitrary"), vmem_limit_bytes=VMEM_LIMIT)
    feat_blk = pl.BlockSpec((1, n_blk, 256, TOK), lambda b, gI: (b, 0, gI, 0))
    ot_shape = jax.ShapeDtypeStruct((batch, n_blk, 512, TOK), BF16)
    score_scratch = [pltpu.VMEM((2, 4, TOK, TOK), F32), pltpu.VMEM((2, 4, 1, TOK), F32)]

    seq256 = pl.BlockSpec((seq, 256), lambda b, gI: (b, gI))
    diff_units = ((0, 0, 0, 128), (1, 0, 0, 128), (0, 128, 128, 128), (1, 128, 128, 128))
    odt = pl.pallas_call(
        functools.partial(_attn_kernel, units=diff_units, diff=True, lambda_init=lambda_init),
        grid=(batch, 2),
        in_specs=[pl.BlockSpec(memory_space=pltpu.SMEM), _const_spec((DIFF_V_DIM, 1)),
                  seq256, seq256, seq256, feat_blk],
        out_specs=feat_blk,
        out_shape=ot_shape,
        scratch_shapes=[pltpu.VMEM((4, 1, TOK), F32), pltpu.VMEM((4, 1, TOK), F32),
                        pltpu.VMEM((4, DIFF_V_DIM, TOK), F32)] + score_scratch,
        compiler_params=params2,
        name="diff_attn",
    )(lam_full, diff_subln_g.reshape(DIFF_V_DIM, 1), dq0, dq1, dk, dvt)

    seq512 = pl.BlockSpec((seq, 512), lambda b, gI: (b, gI))
    mla_units = tuple((0, u * LANES, u * MLA_V, MLA_V) for u in range(4))
    omt = pl.pallas_call(
        functools.partial(_attn_kernel, units=mla_units, diff=False, lambda_init=lambda_init),
        grid=(batch, 2),
        in_specs=[seq512, seq512, feat_blk],
        out_specs=feat_blk,
        out_shape=ot_shape,
        scratch_shapes=[pltpu.VMEM((4, 1, TOK), F32), pltpu.VMEM((4, 1, TOK), F32),
                        pltpu.VMEM((4, MLA_V, TOK), F32)] + score_scratch,
        compiler_params=params2,
        name="mla_attn",
    )(qm, km, vmt)

    wod, wom, wo = w_o_diff.astype(BF16), w_o_mla.astype(BF16), w_out.astype(BF16)
    wup, wdn = w_up.astype(BF16), w_down.astype(BF16)
    row = lambda v: v.reshape(1, D_MODEL)
    out = pl.pallas_call(
        functools.partial(_out_mlp_kernel, alpha=alpha),
        grid=(n_tiles,),
        in_specs=[tok_spec(D_MODEL), feat_spec, feat_spec, tok_spec(2048),
                  _const_spec(wod.shape), _const_spec(wom.shape), _const_spec(wo.shape),
                  _const_spec((1, D_MODEL)), _const_spec((1, D_MODEL)),
                  _const_spec(wup.shape), _const_spec(wdn.shape),
                  _const_spec((1, D_MODEL)), _const_spec((1, D_MODEL))],
        out_specs=tok_spec(D_MODEL),
        out_shape=jax.ShapeDtypeStruct((n_tok, D_MODEL), F32),
        compiler_params=params1,
        name="out_mlp",
    )(x2, odt, omt, g, wod, wom, wo, row(ln1_g), row(ln1_b), wup, wdn, row(ln2_g), row(ln2_b))
    return out


def kernel(x, positions, w_in, gate_b, diff_lambda, diff_subln_g, mla_q_norm_g, w_uq, mla_kv_norm_g, w_ukv,
           w_o_diff, w_o_mla, w_out, ln1_g, ln1_b, w_up, w_down, ln2_g, ln2_b):
    batch, seq, d = x.shape
    depth = w_in.shape[0]
    alpha = (2.0 * depth) ** 0.25
    tab = _rope_freqs()
    x2 = x.reshape(batch * seq, d)
    pos2 = positions.reshape(batch * seq // TOK, 1, TOK)
    for l in range(depth):
        lambda_init = 0.8 - 0.6 * math.exp(-0.3 * l)
        x2 = _layer(x2, pos2, tab, w_in[l], gate_b[l], diff_lambda[l], diff_subln_g[l], mla_q_norm_g[l], w_uq[l],
                    mla_kv_norm_g[l], w_ukv[l], w_o_diff[l], w_o_mla[l], w_out[l], ln1_g[l], ln1_b[l],
                    w_up[l], w_down[l], ln2_g[l], ln2_b[l],
                    batch=batch, seq=seq, lambda_init=lambda_init, alpha=alpha)
    return x2.reshape(batch, seq, d)
```

```python
import functools
import math

import jax
import jax.numpy as jnp
import numpy as np
from jax import lax
from jax.experimental import pallas as pl
from jax.experimental.pallas import tpu as pltpu

F32 = jnp.float32
BF16 = jnp.bfloat16

D_MODEL = 1024
DIFF_HEADS = 4
DIFF_HEAD_DIM = 64
DIFF_V_DIM = 128
DIFF_ROT = 16
MLA_HEADS = 8
MLA_NOPE = 64
MLA_ROPE = 32
MLA_V = 64
MLA_Q_RANK = 384
MLA_KV_RANK = 256
D_FF = 4096
ROPE_THETA = 500000.0
LN_EPS = 1e-5
RMS_EPS = 1e-6
MASK_VALUE = -1e30

LANES = 128
TOK = 512
VMEM_LIMIT = 56 * 1024 * 1024

LOG2E = math.log2(math.e)
Q_SCALE_DIFF = (DIFF_HEAD_DIM ** -0.5) * LOG2E
Q_SCALE_MLA = ((MLA_NOPE + MLA_ROPE) ** -0.5) * LOG2E

_NT = (((1,), (1,)), ((), ()))
_TN = (((0,), (0,)), ((), ()))


def _const_spec(shape):
    n = len(shape)
    return pl.BlockSpec(shape, lambda *_: (0,) * n, pipeline_mode=pl.Buffered(1))


def _rope_freqs():
    half_d, half_m = DIFF_ROT // 2, MLA_ROPE // 2
    inv_d = jnp.power(ROPE_THETA, -jnp.arange(half_d, dtype=F32) / half_d)
    inv_m = jnp.power(ROPE_THETA, -jnp.arange(half_m, dtype=F32) / half_m)
    return jnp.concatenate([inv_d, inv_m, jnp.zeros((32 - half_d - half_m,), F32)]).reshape(32, 1)


def _rope_lane_tables(pos_row, freq_col):
    n = pos_row.shape[1]
    ang = freq_col * pos_row
    c, s = jnp.cos(ang), jnp.sin(ang)
    cd, sd, cm, sm = c[0:8], s[0:8], c[8:24], s[8:24]
    ones = lambda r: jnp.ones((r, n), F32)
    zeros = lambda r: jnp.zeros((r, n), F32)
    rows = lambda *parts: jnp.concatenate(parts, axis=0).T
    cos_d = rows(cd, cd, ones(48), cd, cd, ones(48))
    lo_d = rows(-sd, zeros(56), -sd, zeros(56))
    hi_d = rows(zeros(8), sd, zeros(56), sd, zeros(48))
    cos_m = rows(ones(64), cm, cm, ones(32))
    lo_m = rows(zeros(64), -sm, zeros(48))
    hi_m = rows(zeros(80), sm, zeros(32))
    return (cos_d, lo_d, hi_d), (cos_m, lo_m, hi_m)


def _rope_slab(z, cos, sin_lo, sin_hi, half):
    up = pltpu.roll(z, LANES - half, 1)
    dn = pltpu.roll(z, half, 1)
    return z * cos + up * sin_lo + dn * sin_hi


def _in_proj_kernel(x_ref, pos_ref, tab_ref, wqk_ref, wvt_ref, wc_ref, wg_ref, wuq_ref, wuk_ref, wuvt_ref,
                    gb_ref, qng_ref, kvng_ref,
                    dq0_ref, dq1_ref, dk_ref, dvt_ref, qm_ref, km_ref, vmt_ref, g_ref):
    xb = x_ref[...].astype(BF16)
    pos = pos_ref[0].astype(F32)
    (cos_d, lo_d, hi_d), (cos_m, lo_m, hi_m) = _rope_lane_tables(pos, tab_ref[...])

    first_half = lax.broadcasted_iota(jnp.int32, (TOK, LANES), 1) < DIFF_HEAD_DIM

    zq = jnp.dot(xb, wqk_ref[:, 0:512], preferred_element_type=F32)
    for h in range(DIFF_HEADS):
        sl = slice(h * LANES, (h + 1) * LANES)
        r = _rope_slab(zq[:, sl], cos_d, lo_d, hi_d, DIFF_ROT // 2) * Q_SCALE_DIFF
        dq0_ref[:, sl] = jnp.where(first_half, r, 0.0).astype(BF16)
        dq1_ref[:, sl] = jnp.where(first_half, 0.0, r).astype(BF16)
    zk = jnp.dot(xb, wqk_ref[:, 512:1024], preferred_element_type=F32)
    for h in range(DIFF_HEADS):
        sl = slice(h * LANES, (h + 1) * LANES)
        dk_ref[:, sl] = _rope_slab(zk[:, sl], cos_d, lo_d, hi_d, DIFF_ROT // 2).astype(BF16)
    dvt_ref[0, 0] = lax.dot_general(wvt_ref[...], xb, _NT, preferred_element_type=F32).astype(BF16)

    zc = jnp.dot(xb, wc_ref[...], preferred_element_type=F32)
    cq = zc[:, 0:MLA_Q_RANK]
    ckv = zc[:, MLA_Q_RANK:MLA_Q_RANK + MLA_KV_RANK]
    kr = zc[:, MLA_Q_RANK + MLA_KV_RANK:]
    cqn = cq * lax.rsqrt(jnp.mean(cq * cq, axis=-1, keepdims=True) + RMS_EPS) * qng_ref[...]
    ckvn = (ckv * lax.rsqrt(jnp.mean(ckv * ckv, axis=-1, keepdims=True) + RMS_EPS) * kvng_ref[...]).astype(BF16)
    q = jnp.dot(cqn.astype(BF16), wuq_ref[...], preferred_element_type=F32)
    kn = jnp.dot(ckvn, wuk_ref[...], preferred_element_type=F32)
    kr_rot = _rope_slab(kr, cos_m, lo_m, hi_m, MLA_ROPE // 2)
    for h in range(MLA_HEADS):
        sl = slice(h * LANES, (h + 1) * LANES)
        qm_ref[:, sl] = (_rope_slab(q[:, sl], cos_m, lo_m, hi_m, MLA_ROPE // 2) * Q_SCALE_MLA).astype(BF16)
        km_ref[:, sl] = (kn[:, sl] + kr_rot).astype(BF16)
    vmt_ref[0, 0] = lax.dot_general(wuvt_ref[...], ckvn, _NT, preferred_element_type=F32).astype(BF16)

    for c in range(4):
        sl = slice(c * 512, (c + 1) * 512)
        zg = jnp.dot(xb, wg_ref[:, sl], preferred_element_type=F32) + gb_ref[:, sl]
        g_ref[:, sl] = jax.nn.sigmoid(zg).astype(BF16)


def _attn_kernel(*refs, units, diff, lambda_init):
    if diff:
        lam_ref, gcol_ref, q0_ref, q1_ref, k_ref, vt_ref, ot_ref = refs[:7]
        q_refs = (q0_ref, q1_ref)
    else:
        q0_ref, k_ref, vt_ref, ot_ref = refs[:4]
        q_refs = (q0_ref,)
    m_sc, l_sc, acc_sc, s_sc, mb_sc, tri_sc = refs[-6:]
    n_blk = vt_ref.shape[1]
    n_units = len(units)

    key = lax.broadcasted_iota(jnp.int32, tri_sc.shape, 0)
    qry = lax.broadcasted_iota(jnp.int32, tri_sc.shape, 1)
    tri_sc[...] = jnp.where(key <= qry, 0.0, MASK_VALUE).astype(F32)

    def scores(u, qi, kj, slot, masked):
        qr, col, _, _ = units[u]
        q = q_refs[qr][pl.ds(qi * TOK, TOK), col:col + LANES]
        k = k_ref[pl.ds(kj * TOK, TOK), col:col + LANES]
        if not masked:
            s = lax.dot_general(k, q, _NT, preferred_element_type=F32)
            s_sc[slot, u] = s
            mb_sc[slot, u] = jnp.max(s, axis=0, keepdims=True)
            return
        h = TOK // 2
        lo, hi = slice(0, h), slice(h, TOK)
        s00 = lax.dot_general(k[lo], q[lo], _NT, preferred_element_type=F32) + tri_sc[...]
        s01 = lax.dot_general(k[lo], q[hi], _NT, preferred_element_type=F32)
        s11 = lax.dot_general(k[hi], q[hi], _NT, preferred_element_type=F32) + tri_sc[...]
        s_sc[slot, u, lo, lo] = s00
        s_sc[slot, u, lo, hi] = s01
        s_sc[slot, u, hi, lo] = jnp.full((h, h), MASK_VALUE, F32)
        s_sc[slot, u, hi, hi] = s11
        mb_sc[slot, u, :, lo] = jnp.max(s00, axis=0, keepdims=True)
        mb_sc[slot, u, :, hi] = jnp.maximum(jnp.max(s01, axis=0, keepdims=True), jnp.max(s11, axis=0, keepdims=True))

    def accumulate(u, kj, slot):
        _, _, vrow, vrows = units[u]
        m_old = m_sc[u]
        m_new = jnp.maximum(m_old, mb_sc[slot, u])
        alpha = jnp.exp2(m_old - m_new)
        p = jnp.exp2(s_sc[slot, u] - m_new)
        l_sc[u] = alpha * l_sc[u] + jnp.sum(p, axis=0, keepdims=True)
        m_sc[u] = m_new
        v = vt_ref[0, kj, vrow:vrow + vrows, :]
        pv = jnp.dot(v, p.astype(BF16), preferred_element_type=F32)
        acc_sc[u, 0:vrows, :] = alpha * acc_sc[u, 0:vrows, :] + pv

    def finish(u, qi):
        _, _, vrow, vrows = units[u]
        if diff:
            if u % 2 == 0:
                return
            o = acc_sc[u - 1] * (1.0 / l_sc[u - 1]) - lam_ref[0] * (acc_sc[u] * (1.0 / l_sc[u]))
            ms = jnp.mean(o * o, axis=0, keepdims=True)
            o = o * lax.rsqrt(ms + RMS_EPS) * gcol_ref[...] * (1.0 - lambda_init)
            group = (u - 1, u)
        else:
            o = acc_sc[u, 0:vrows, :] * (1.0 / l_sc[u])
            group = (u,)
        ot_ref[0, qi, vrow:vrow + vrows, :] = o.astype(BF16)
        for g in group:
            m_sc[g] = jnp.full((1, TOK), MASK_VALUE, F32)

    def step(qi, kj_next, kj_cur, cur, masked, finish_qi=None):
        for u in range(n_units):
            scores(u, qi, kj_next, 1 - cur, masked)
            accumulate(u, kj_cur, cur)
            if finish_qi is not None:
                finish(u, finish_qi)

    m_sc[...] = jnp.full(m_sc.shape, MASK_VALUE, F32)
    l_sc[...] = jnp.zeros(l_sc.shape, F32)
    acc_sc[...] = jnp.zeros(acc_sc.shape, F32)
    for u in range(n_units):
        scores(u, 0, 0, 0, True)

    def q_block_body(qi, d):
        step(qi, qi, jnp.maximum(qi - 2, 0), 1 - d, True, finish_qi=qi - 1)

        def k_pair(t, c):
            j = 2 * t
            step(qi, j, jnp.where(t == 0, qi, j - 1), d, False)
            step(qi, j + 1, j, 1 - d, False)
            return c

        lax.fori_loop(0, qi // 2, k_pair, 0)

        @pl.when(qi % 2 == 1)
        def _():
            step(qi, qi - 1, jnp.where(qi == 1, qi, qi - 2), d, False)

    def q_block(qi, c):
        diag_in_slot1 = ((qi + 1) // 2) % 2 == 1

        @pl.when(diag_in_slot1)
        def _():
            q_block_body(qi, 1)

        @pl.when(jnp.logical_not(diag_in_slot1))
        def _():
            q_block_body(qi, 0)

        return c

    lax.fori_loop(1, n_blk, q_block, 0)
    last_slot = 0 if (n_blk * (n_blk + 1) // 2 - 1) % 2 == 0 else 1
    for u in range(n_units):
        accumulate(u, n_blk - 2, last_slot)
        finish(u, n_blk - 1)


def _layer_norm(r, g, b):
    mu = jnp.mean(r, axis=-1, keepdims=True)
    d = r - mu
    var = jnp.mean(d * d, axis=-1, keepdims=True)
    return d * lax.rsqrt(var + LN_EPS) * g + b


def _out_mlp_kernel(x_ref, od_ref, om_ref, g_ref, wod_ref, wom_ref, wout_ref, l1g_ref, l1b_ref,
                    wup_ref, wdn_ref, l2g_ref, l2b_ref, o_ref, *, alpha):
    half = TOK // 2
    rows = (slice(0, half), slice(half, TOK))

    def mixer_residual(r):
        yd = lax.dot_general(od_ref[0, 0, :, r], wod_ref[...], _TN, preferred_element_type=F32)
        ym = lax.dot_general(om_ref[0, 0, :, r], wom_ref[...], _TN, preferred_element_type=F32)
        y = g_ref[r, 0:D_MODEL].astype(F32) * yd + g_ref[r, D_MODEL:].astype(F32) * ym
        return alpha * x_ref[r, :] + jnp.dot(y.astype(BF16), wout_ref[...], preferred_element_type=F32)

    def mlp_hidden(x1):
        u = jnp.dot(x1.astype(BF16), wup_ref[...], preferred_element_type=F32)
        u = jnp.maximum(u, 0.0)
        return (u * u).astype(BF16)

    r_a, r_b = mixer_residual(rows[0]), mixer_residual(rows[1])
    x1_a = _layer_norm(r_a, l1g_ref[...], l1b_ref[...])
    u_a = mlp_hidden(x1_a)
    x1_b = _layer_norm(r_b, l1g_ref[...], l1b_ref[...])
    u_b = mlp_hidden(x1_b)
    f_a = jnp.dot(u_a, wdn_ref[...], preferred_element_type=F32)
    f_b = jnp.dot(u_b, wdn_ref[...], preferred_element_type=F32)
    o_ref[rows[0], :] = _layer_norm(alpha * x1_a + f_a, l2g_ref[...], l2b_ref[...])
    o_ref[rows[1], :] = _layer_norm(alpha * x1_b + f_b, l2g_ref[...], l2b_ref[...])


def _layer(x2, pos2, tab, w_in, gate_b, diff_lambda, diff_subln_g, mla_q_norm_g, w_uq, mla_kv_norm_g, w_ukv,
           w_o_diff, w_o_mla, w_out, ln1_g, ln1_b, w_up, w_down, ln2_g, ln2_b, *, batch, seq, lambda_init, alpha):
    n_tok = batch * seq
    n_blk = seq // TOK
    n_tiles = n_tok // TOK

    c0, c1, c2, c3, c4, c5 = 512, 1024, 1536, 1920, 2176, 2208
    wqk = w_in[:, 0:c1].astype(BF16)
    wvt = w_in[:, c1:c2].T.astype(BF16)
    kr_pad = jnp.zeros((D_MODEL, LANES), F32).at[:, MLA_NOPE:MLA_NOPE + MLA_ROPE].set(w_in[:, c4:c5])
    wc = jnp.concatenate([w_in[:, c2:c4], kr_pad], axis=1).astype(BF16)
    wg = w_in[:, c5:].astype(BF16)
    wuq3 = w_uq.reshape(MLA_Q_RANK, MLA_HEADS, MLA_NOPE + MLA_ROPE)
    wuq = jnp.pad(wuq3, ((0, 0), (0, 0), (0, LANES - MLA_NOPE - MLA_ROPE))).reshape(MLA_Q_RANK, MLA_HEADS * LANES)
    wukv3 = w_ukv.reshape(MLA_KV_RANK, MLA_HEADS, MLA_NOPE + MLA_V)
    wuk = jnp.pad(wukv3[:, :, :MLA_NOPE], ((0, 0), (0, 0), (0, LANES - MLA_NOPE))).reshape(MLA_KV_RANK, MLA_HEADS * LANES)
    wuvt = wukv3[:, :, MLA_NOPE:].reshape(MLA_KV_RANK, MLA_HEADS * MLA_V).T
    wuq, wuk, wuvt = wuq.astype(BF16), wuk.astype(BF16), wuvt.astype(BF16)
    gb = gate_b.reshape(1, 2 * D_MODEL)
    lam = diff_lambda.astype(F32)
    lam_full = (jnp.exp(jnp.sum(lam[0] * lam[1])) - jnp.exp(jnp.sum(lam[2] * lam[3])) + lambda_init).reshape(1)

    tok_spec = lambda w: pl.BlockSpec((TOK, w), lambda i: (i, 0))
    feat_spec = pl.BlockSpec((1, 1, 512, TOK), lambda i: (i // n_blk, i % n_blk, 0, 0))
    params1 = pltpu.CompilerParams(dimension_semantics=("arbitrary",), vmem_limit_bytes=VMEM_LIMIT)

    dq0, dq1, dk, dvt, qm, km, vmt, g = pl.pallas_call(
        _in_proj_kernel,
        grid=(n_tiles,),
        in_specs=[tok_spec(D_MODEL), pl.BlockSpec((1, 1, TOK), lambda i: (i, 0, 0)), _const_spec((32, 1)),
                  _const_spec(wqk.shape), _const_spec(wvt.shape), _const_spec(wc.shape), _const_spec(wg.shape),
                  _const_spec(wuq.shape), _const_spec(wuk.shape), _const_spec(wuvt.shape),
                  _const_spec(gb.shape), _const_spec((1, MLA_Q_RANK)), _const_spec((1, MLA_KV_RANK))],
        out_specs=[tok_spec(512), tok_spec(512), tok_spec(512), feat_spec,
                   tok_spec(1024), tok_spec(1024), feat_spec, tok_spec(2048)],
        out_shape=[jax.ShapeDtypeStruct((n_tok, 512), BF16), jax.ShapeDtypeStruct((n_tok, 512), BF16),
                   jax.ShapeDtypeStruct((n_tok, 512), BF16), jax.ShapeDtypeStruct((batch, n_blk, 512, TOK), BF16),
                   jax.ShapeDtypeStruct((n_tok, 1024), BF16), jax.ShapeDtypeStruct((n_tok, 1024), BF16),
                   jax.ShapeDtypeStruct((batch, n_blk, 512, TOK), BF16), jax.ShapeDtypeStruct((n_tok, 2048), BF16)],
        compiler_params=params1,
        name="in_proj",
    )(x2, pos2, tab, wqk, wvt, wc, wg, wuq, wuk, wuvt, gb,
      mla_q_norm_g.reshape(1, MLA_Q_RANK), mla_kv_norm_g.reshape(1, MLA_KV_RANK))

    params2 = pltpu.CompilerParams(dimension_semantics=("arbitrary", "arbitrary"), vmem_limit_bytes=VMEM_LIMIT)
    feat_blk = pl.BlockSpec((1, n_blk, 256, TOK), lambda b, gI: (b, 0, gI, 0))
    ot_shape = jax.ShapeDtypeStruct((batch, n_blk, 512, TOK), BF16)
    score_scratch = [pltpu.VMEM((2, 4, TOK, TOK), F32), pltpu.VMEM((2, 4, 1, TOK), F32),
                     pltpu.VMEM((TOK // 2, TOK // 2), F32)]

    seq256 = pl.BlockSpec((seq, 256), lambda b, gI: (b, gI))
    diff_units = ((0, 0, 0, 128), (1, 0, 0, 128), (0, 128, 128, 128), (1, 128, 128, 128))
    odt = pl.pallas_call(
        functools.partial(_attn_kernel, units=diff_units, diff=True, lambda_init=lambda_init),
        grid=(batch, 2),
        in_specs=[pl.BlockSpec(memory_space=pltpu.SMEM), _const_spec((DIFF_V_DIM, 1)),
                  seq256, seq256, seq256, feat_blk],
        out_specs=feat_blk,
        out_shape=ot_shape,
        scratch_shapes=[pltpu.VMEM((4, 1, TOK), F32), pltpu.VMEM((4, 1, TOK), F32),
                        pltpu.VMEM((4, DIFF_V_DIM, TOK), F32)] + score_scratch,
        compiler_params=params2,
        name="diff_attn",
    )(lam_full, diff_subln_g.reshape(DIFF_V_DIM, 1), dq0, dq1, dk, dvt)

    seq512 = pl.BlockSpec((seq, 512), lambda b, gI: (b, gI))
    mla_units = tuple((0, u * LANES, u * MLA_V, MLA_V) for u in range(4))
    omt = pl.pallas_call(
        functools.partial(_attn_kernel, units=mla_units, diff=False, lambda_init=lambda_init),
        grid=(batch, 2),
        in_specs=[seq512, seq512, feat_blk],
        out_specs=feat_blk,
        out_shape=ot_shape,
        scratch_shapes=[pltpu.VMEM((4, 1, TOK), F32), pltpu.VMEM((4, 1, TOK), F32),
                        pltpu.VMEM((4, MLA_V, TOK), F32)] + score_scratch,
        compiler_params=params2,
        name="mla_attn",
    )(qm, km, vmt)

    wod, wom, wo = w_o_diff.astype(BF16), w_o_mla.astype(BF16), w_out.astype(BF16)
    wup, wdn = w_up.astype(BF16), w_down.astype(BF16)
    row = lambda v: v.reshape(1, D_MODEL)
    out = pl.pallas_call(
        functools.partial(_out_mlp_kernel, alpha=alpha),
        grid=(n_tiles,),
        in_specs=[tok_spec(D_MODEL), feat_spec, feat_spec, tok_spec(2048),
                  _const_spec(wod.shape), _const_spec(wom.shape), _const_spec(wo.shape),
                  _const_spec((1, D_MODEL)), _const_spec((1, D_MODEL)),
                  _const_spec(wup.shape), _const_spec(wdn.shape),
                  _const_spec((1, D_MODEL)), _const_spec((1, D_MODEL))],
        out_specs=tok_spec(D_MODEL),
        out_shape=jax.ShapeDtypeStruct((n_tok, D_MODEL), F32),
        compiler_params=params1,
        name="out_mlp",
    )(x2, odt, omt, g, wod, wom, wo, row(ln1_g), row(ln1_b), wup, wdn, row(ln2_g), row(ln2_b))
    return out


def kernel(x, positions, w_in, gate_b, diff_lambda, diff_subln_g, mla_q_norm_g, w_uq, mla_kv_norm_g, w_ukv,
           w_o_diff, w_o_mla, w_out, ln1_g, ln1_b, w_up, w_down, ln2_g, ln2_b):
    batch, seq, d = x.shape
    depth = w_in.shape[0]
    alpha = (2.0 * depth) ** 0.25
    tab = _rope_freqs()
    x2 = x.reshape(batch * seq, d)
    pos2 = positions.reshape(batch * seq // TOK, 1, TOK)
    for l in range(depth):
        lambda_init = 0.8 - 0.6 * math.exp(-0.3 * l)
        x2 = _layer(x2, pos2, tab, w_in[l], gate_b[l], diff_lambda[l], diff_subln_g[l], mla_q_norm_g[l], w_uq[l],
                    mla_kv_norm_g[l], w_ukv[l], w_o_diff[l], w_o_mla[l], w_out[l], ln1_g[l], ln1_b[l],
                    w_up[l], w_down[l], ln2_g[l], ln2_b[l],
                    batch=batch, seq=seq, lambda_init=lambda_init, alpha=alpha)
    return x2.reshape(batch, seq, d)
```

```python
import functools
import math

import jax
import jax.numpy as jnp
import numpy as np
from jax import lax
from jax.experimental import pallas as pl
from jax.experimental.pallas import tpu as pltpu

F32 = jnp.float32
BF16 = jnp.bfloat16

D_MODEL = 1024
DIFF_HEADS = 4
DIFF_HEAD_DIM = 64
DIFF_V_DIM = 128
DIFF_ROT = 16
MLA_HEADS = 8
MLA_NOPE = 64
MLA_ROPE = 32
MLA_V = 64
MLA_Q_RANK = 384
MLA_KV_RANK = 256
D_FF = 4096
ROPE_THETA = 500000.0
LN_EPS = 1e-5
RMS_EPS = 1e-6
MASK_VALUE = -1e30

LANES = 128
TOK = 512
PROJ_TOK = 512
VMEM_LIMIT = 56 * 1024 * 1024

LOG2E = math.log2(math.e)
Q_SCALE_DIFF = (DIFF_HEAD_DIM ** -0.5) * LOG2E
Q_SCALE_MLA = ((MLA_NOPE + MLA_ROPE) ** -0.5) * LOG2E

_NT = (((1,), (1,)), ((), ()))
_TN = (((0,), (0,)), ((), ()))


def _const_spec(shape):
    n = len(shape)
    return pl.BlockSpec(shape, lambda *_: (0,) * n, pipeline_mode=pl.Buffered(1))


def _rope_freqs():
    half_d, half_m = DIFF_ROT // 2, MLA_ROPE // 2
    inv_d = jnp.power(ROPE_THETA, -jnp.arange(half_d, dtype=F32) / half_d)
    inv_m = jnp.power(ROPE_THETA, -jnp.arange(half_m, dtype=F32) / half_m)
    return jnp.concatenate([inv_d, inv_m, jnp.zeros((32 - half_d - half_m,), F32)]).reshape(32, 1)


def _rope_lane_tables(pos_row, freq_col):
    n = pos_row.shape[1]
    ang = freq_col * pos_row
    c, s = jnp.cos(ang), jnp.sin(ang)
    cd, sd, cm, sm = c[0:8], s[0:8], c[8:24], s[8:24]
    ones = lambda r: jnp.ones((r, n), F32)
    zeros = lambda r: jnp.zeros((r, n), F32)
    rows = lambda *parts: jnp.concatenate(parts, axis=0).T
    cos_d = rows(cd, cd, ones(48), cd, cd, ones(48))
    lo_d = rows(-sd, zeros(56), -sd, zeros(56))
    hi_d = rows(zeros(8), sd, zeros(56), sd, zeros(48))
    cos_m = rows(ones(64), cm, cm, ones(32))
    lo_m = rows(zeros(64), -sm, zeros(48))
    hi_m = rows(zeros(80), sm, zeros(32))
    return (cos_d, lo_d, hi_d), (cos_m, lo_m, hi_m)


def _rope_slab(z, cos, sin_lo, sin_hi, half):
    up = pltpu.roll(z, LANES - half, 1)
    dn = pltpu.roll(z, half, 1)
    return z * cos + up * sin_lo + dn * sin_hi


def _in_proj_kernel(x_ref, pos_ref, tab_ref, wqk_ref, wvt_ref, wc_ref, wg_ref, wuq_ref, wuk_ref, wuvt_ref,
                    gb_ref, qng_ref, kvng_ref,
                    dq0_ref, dq1_ref, dk_ref, dvt_ref, qm_ref, km_ref, vmt_ref, g_ref):
    xb = x_ref[...].astype(BF16)
    pos = pos_ref[0].astype(F32)
    (cos_d, lo_d, hi_d), (cos_m, lo_m, hi_m) = _rope_lane_tables(pos, tab_ref[...])

    first_half = lax.broadcasted_iota(jnp.int32, (x_ref.shape[0], LANES), 1) < DIFF_HEAD_DIM

    zq = jnp.dot(xb, wqk_ref[:, 0:512], preferred_element_type=F32)
    for h in range(DIFF_HEADS):
        sl = slice(h * LANES, (h + 1) * LANES)
        r = _rope_slab(zq[:, sl], cos_d, lo_d, hi_d, DIFF_ROT // 2) * Q_SCALE_DIFF
        dq0_ref[:, sl] = jnp.where(first_half, r, 0.0).astype(BF16)
        dq1_ref[:, sl] = jnp.where(first_half, 0.0, r).astype(BF16)
    zk = jnp.dot(xb, wqk_ref[:, 512:1024], preferred_element_type=F32)
    for h in range(DIFF_HEADS):
        sl = slice(h * LANES, (h + 1) * LANES)
        dk_ref[:, sl] = _rope_slab(zk[:, sl], cos_d, lo_d, hi_d, DIFF_ROT // 2).astype(BF16)
    dvt_ref[0, 0] = lax.dot_general(wvt_ref[...], xb, _NT, preferred_element_type=F32).astype(BF16)

    zc = jnp.dot(xb, wc_ref[...], preferred_element_type=F32)
    cq = zc[:, 0:MLA_Q_RANK]
    ckv = zc[:, MLA_Q_RANK:MLA_Q_RANK + MLA_KV_RANK]
    kr = zc[:, MLA_Q_RANK + MLA_KV_RANK:]
    cqn = cq * lax.rsqrt(jnp.mean(cq * cq, axis=-1, keepdims=True) + RMS_EPS) * qng_ref[...]
    ckvn = (ckv * lax.rsqrt(jnp.mean(ckv * ckv, axis=-1, keepdims=True) + RMS_EPS) * kvng_ref[...]).astype(BF16)
    q = jnp.dot(cqn.astype(BF16), wuq_ref[...], preferred_element_type=F32)
    kn = jnp.dot(ckvn, wuk_ref[...], preferred_element_type=F32)
    kr_rot = _rope_slab(kr, cos_m, lo_m, hi_m, MLA_ROPE // 2)
    for h in range(MLA_HEADS):
        sl = slice(h * LANES, (h + 1) * LANES)
        qm_ref[:, sl] = (_rope_slab(q[:, sl], cos_m, lo_m, hi_m, MLA_ROPE // 2) * Q_SCALE_MLA).astype(BF16)
        km_ref[:, sl] = (kn[:, sl] + kr_rot).astype(BF16)
    vmt_ref[0, 0] = lax.dot_general(wuvt_ref[...], ckvn, _NT, preferred_element_type=F32).astype(BF16)

    for c in range(4):
        sl = slice(c * 512, (c + 1) * 512)
        zg = jnp.dot(xb, wg_ref[:, sl], preferred_element_type=F32) + gb_ref[:, sl]
        g_ref[:, sl] = jax.nn.sigmoid(zg).astype(BF16)


def _attn_kernel(*refs, units, diff, lambda_init):
    if diff:
        lam_ref, gcol_ref, q0_ref, q1_ref, k_ref, vt_ref, ot_ref = refs[:7]
        q_refs = (q0_ref, q1_ref)
    else:
        q0_ref, k_ref, vt_ref, ot_ref = refs[:4]
        q_refs = (q0_ref,)
    m_sc, l_sc, acc_sc, s_sc, mb_sc, tri_sc = refs[-6:]
    n_blk = vt_ref.shape[1]
    n_units = len(units)

    key = lax.broadcasted_iota(jnp.int32, tri_sc.shape, 0)
    qry = lax.broadcasted_iota(jnp.int32, tri_sc.shape, 1)
    tri_sc[...] = jnp.where(key <= qry, 0.0, MASK_VALUE).astype(F32)

    def scores(u, qi, kj, slot, masked):
        qr, col, _, _ = units[u]
        q = q_refs[qr][pl.ds(qi * TOK, TOK), col:col + LANES]
        k = k_ref[pl.ds(kj * TOK, TOK), col:col + LANES]
        if not masked:
            s = lax.dot_general(k, q, _NT, preferred_element_type=F32)
            s_sc[slot, u] = s
            mb_sc[slot, u] = jnp.max(s, axis=0, keepdims=True)
            return
        h = TOK // 2
        lo, hi = slice(0, h), slice(h, TOK)
        s00 = lax.dot_general(k[lo], q[lo], _NT, preferred_element_type=F32) + tri_sc[...]
        s01 = lax.dot_general(k[lo], q[hi], _NT, preferred_element_type=F32)
        s11 = lax.dot_general(k[hi], q[hi], _NT, preferred_element_type=F32) + tri_sc[...]
        s_sc[slot, u, lo, lo] = s00
        s_sc[slot, u, lo, hi] = s01
        s_sc[slot, u, hi, lo] = jnp.full((h, h), MASK_VALUE, F32)
        s_sc[slot, u, hi, hi] = s11
        mb_sc[slot, u, :, lo] = jnp.max(s00, axis=0, keepdims=True)
        mb_sc[slot, u, :, hi] = jnp.maximum(jnp.max(s01, axis=0, keepdims=True), jnp.max(s11, axis=0, keepdims=True))

    def accumulate(u, kj, slot):
        _, _, vrow, vrows = units[u]
        m_old = m_sc[u]
        m_new = jnp.maximum(m_old, mb_sc[slot, u])
        alpha = jnp.exp2(m_old - m_new)
        p = jnp.exp2(s_sc[slot, u] - m_new)
        l_sc[u] = alpha * l_sc[u] + jnp.sum(p, axis=0, keepdims=True)
        m_sc[u] = m_new
        v = vt_ref[0, kj, vrow:vrow + vrows, :]
        pv = jnp.dot(v, p.astype(BF16), preferred_element_type=F32)
        acc_sc[u, 0:vrows, :] = alpha * acc_sc[u, 0:vrows, :] + pv

    def finish(u, qi):
        _, _, vrow, vrows = units[u]
        if diff:
            if u % 2 == 0:
                return
            o = acc_sc[u - 1] * (1.0 / l_sc[u - 1]) - lam_ref[0] * (acc_sc[u] * (1.0 / l_sc[u]))
            ms = jnp.mean(o * o, axis=0, keepdims=True)
            o = o * lax.rsqrt(ms + RMS_EPS) * gcol_ref[...] * (1.0 - lambda_init)
            group = (u - 1, u)
        else:
            o = acc_sc[u, 0:vrows, :] * (1.0 / l_sc[u])
            group = (u,)
        ot_ref[0, qi, vrow:vrow + vrows, :] = o.astype(BF16)
        for g in group:
            m_sc[g] = jnp.full((1, TOK), MASK_VALUE, F32)

    def step(qi, kj_next, kj_cur, cur, masked, finish_qi=None):
        for u in range(n_units):
            scores(u, qi, kj_next, 1 - cur, masked)
            accumulate(u, kj_cur, cur)
            if finish_qi is not None:
                finish(u, finish_qi)

    m_sc[...] = jnp.full(m_sc.shape, MASK_VALUE, F32)
    l_sc[...] = jnp.zeros(l_sc.shape, F32)
    acc_sc[...] = jnp.zeros(acc_sc.shape, F32)
    for u in range(n_units):
        scores(u, 0, 0, 0, True)

    def q_block_body(qi, d):
        step(qi, qi, jnp.maximum(qi - 2, 0), 1 - d, True, finish_qi=qi - 1)
        step(qi, 0, qi, d, False)

        def run(j0, n):
            for i in range(n):
                step(qi, j0 + i, j0 + i - 1, 1 - d if i % 2 == 0 else d, False)

        def k_quad(t, c):
            run(1 + 4 * t, 4)
            return c

        rest = qi - 1
        lax.fori_loop(0, rest // 4, k_quad, 0)

        @pl.when(rest % 4 >= 2)
        def _():
            run(1 + rest - rest % 4, 2)

        @pl.when(rest % 2 == 1)
        def _():
            run(qi - 1, 1)

    def q_block(qi, c):
        diag_in_slot1 = ((qi + 1) // 2) % 2 == 1

        @pl.when(diag_in_slot1)
        def _():
            q_block_body(qi, 1)

        @pl.when(jnp.logical_not(diag_in_slot1))
        def _():
            q_block_body(qi, 0)

        return c

    lax.fori_loop(1, n_blk, q_block, 0)
    last_slot = 0 if (n_blk * (n_blk + 1) // 2 - 1) % 2 == 0 else 1
    for u in range(n_units):
        accumulate(u, n_blk - 2, last_slot)
        finish(u, n_blk - 1)


def _layer_norm(r, g, b):
    mu = jnp.mean(r, axis=-1, keepdims=True)
    d = r - mu
    var = jnp.mean(d * d, axis=-1, keepdims=True)
    return d * lax.rsqrt(var + LN_EPS) * g + b


def _out_mlp_kernel(x_ref, od_ref, om_ref, g_ref, wod_ref, wom_ref, wout_ref, l1g_ref, l1b_ref,
                    wup_ref, wdn_ref, l2g_ref, l2b_ref, o_ref, *, alpha):
    half = TOK // 2
    rows = (slice(0, half), slice(half, TOK))

    def mixer_residual(r):
        yd = lax.dot_general(od_ref[0, 0, :, r], wod_ref[...], _TN, preferred_element_type=F32)
        ym = lax.dot_general(om_ref[0, 0, :, r], wom_ref[...], _TN, preferred_element_type=F32)
        y = g_ref[r, 0:D_MODEL].astype(F32) * yd + g_ref[r, D_MODEL:].astype(F32) * ym
        return alpha * x_ref[r, :] + jnp.dot(y.astype(BF16), wout_ref[...], preferred_element_type=F32)

    def mlp_hidden(x1):
        u = jnp.dot(x1.astype(BF16), wup_ref[...], preferred_element_type=F32)
        u = jnp.maximum(u, 0.0)
        return (u * u).astype(BF16)

    r_a, r_b = mixer_residual(rows[0]), mixer_residual(rows[1])
    x1_a = _layer_norm(r_a, l1g_ref[...], l1b_ref[...])
    u_a = mlp_hidden(x1_a)
    x1_b = _layer_norm(r_b, l1g_ref[...], l1b_ref[...])
    u_b = mlp_hidden(x1_b)
    f_a = jnp.dot(u_a, wdn_ref[...], preferred_element_type=F32)
    f_b = jnp.dot(u_b, wdn_ref[...], preferred_element_type=F32)
    o_ref[rows[0], :] = _layer_norm(alpha * x1_a + f_a, l2g_ref[...], l2b_ref[...])
    o_ref[rows[1], :] = _layer_norm(alpha * x1_b + f_b, l2g_ref[...], l2b_ref[...])


def _layer(x2, pos2, tab, w_in, gate_b, diff_lambda, diff_subln_g, mla_q_norm_g, w_uq, mla_kv_norm_g, w_ukv,
           w_o_diff, w_o_mla, w_out, ln1_g, ln1_b, w_up, w_down, ln2_g, ln2_b, *, batch, seq, lambda_init, alpha):
    n_tok = batch * seq
    n_blk = seq // TOK
    n_tiles = n_tok // TOK

    c0, c1, c2, c3, c4, c5 = 512, 1024, 1536, 1920, 2176, 2208
    wqk = w_in[:, 0:c1].astype(BF16)
    wvt = w_in[:, c1:c2].T.astype(BF16)
    kr_pad = jnp.zeros((D_MODEL, LANES), F32).at[:, MLA_NOPE:MLA_NOPE + MLA_ROPE].set(w_in[:, c4:c5])
    wc = jnp.concatenate([w_in[:, c2:c4], kr_pad], axis=1).astype(BF16)
    wg = w_in[:, c5:].astype(BF16)
    wuq3 = w_uq.reshape(MLA_Q_RANK, MLA_HEADS, MLA_NOPE + MLA_ROPE)
    wuq = jnp.pad(wuq3, ((0, 0), (0, 0), (0, LANES - MLA_NOPE - MLA_ROPE))).reshape(MLA_Q_RANK, MLA_HEADS * LANES)
    wukv3 = w_ukv.reshape(MLA_KV_RANK, MLA_HEADS, MLA_NOPE + MLA_V)
    wuk = jnp.pad(wukv3[:, :, :MLA_NOPE], ((0, 0), (0, 0), (0, LANES - MLA_NOPE))).reshape(MLA_KV_RANK, MLA_HEADS * LANES)
    wuvt = wukv3[:, :, MLA_NOPE:].reshape(MLA_KV_RANK, MLA_HEADS * MLA_V).T
    wuq, wuk, wuvt = wuq.astype(BF16), wuk.astype(BF16), wuvt.astype(BF16)
    gb = gate_b.reshape(1, 2 * D_MODEL)
    lam = diff_lambda.astype(F32)
    lam_full = (jnp.exp(jnp.sum(lam[0] * lam[1])) - jnp.exp(jnp.sum(lam[2] * lam[3])) + lambda_init).reshape(1)

    tok_spec = lambda w: pl.BlockSpec((TOK, w), lambda i: (i, 0))
    feat_spec = pl.BlockSpec((1, 1, 512, TOK), lambda i: (i // n_blk, i % n_blk, 0, 0))
    params1 = pltpu.CompilerParams(dimension_semantics=("arbitrary",), vmem_limit_bytes=VMEM_LIMIT)

    sub = TOK // PROJ_TOK
    ptok_spec = lambda w: pl.BlockSpec((PROJ_TOK, w), lambda i: (i, 0))
    pfeat_spec = pl.BlockSpec((1, 1, 512, PROJ_TOK),
                              lambda i: (i // (n_blk * sub), (i // sub) % n_blk, 0, i % sub))
    dq0, dq1, dk, dvt, qm, km, vmt, g = pl.pallas_call(
        _in_proj_kernel,
        grid=(n_tiles * sub,),
        in_specs=[ptok_spec(D_MODEL), pl.BlockSpec((1, 1, PROJ_TOK), lambda i: (i, 0, 0)), _const_spec((32, 1)),
                  _const_spec(wqk.shape), _const_spec(wvt.shape), _const_spec(wc.shape), _const_spec(wg.shape),
                  _const_spec(wuq.shape), _const_spec(wuk.shape), _const_spec(wuvt.shape),
                  _const_spec(gb.shape), _const_spec((1, MLA_Q_RANK)), _const_spec((1, MLA_KV_RANK))],
        out_specs=[ptok_spec(512), ptok_spec(512), ptok_spec(512), pfeat_spec,
                   ptok_spec(1024), ptok_spec(1024), pfeat_spec, ptok_spec(2048)],
        out_shape=[jax.ShapeDtypeStruct((n_tok, 512), BF16), jax.ShapeDtypeStruct((n_tok, 512), BF16),
                   jax.ShapeDtypeStruct((n_tok, 512), BF16), jax.ShapeDtypeStruct((batch, n_blk, 512, TOK), BF16),
                   jax.ShapeDtypeStruct((n_tok, 1024), BF16), jax.ShapeDtypeStruct((n_tok, 1024), BF16),
                   jax.ShapeDtypeStruct((batch, n_blk, 512, TOK), BF16), jax.ShapeDtypeStruct((n_tok, 2048), BF16)],
        compiler_params=params1,
        name="in_proj",
    )(x2, pos2, tab, wqk, wvt, wc, wg, wuq, wuk, wuvt, gb,
      mla_q_norm_g.reshape(1, MLA_Q_RANK), mla_kv_norm_g.reshape(1, MLA_KV_RANK))

    params2 = pltpu.CompilerParams(dimension_semantics=("arbitrary", "arbitrary"), vmem_limit_bytes=VMEM_LIMIT)
    feat_blk = pl.BlockSpec((1, n_blk, 256, TOK), lambda b, gI: (b, 0, gI, 0))
    ot_shape = jax.ShapeDtypeStruct((batch, n_blk, 512, TOK), BF16)
    score_scratch = [pltpu.VMEM((2, 4, TOK, TOK), F32), pltpu.VMEM((2, 4, 1, TOK), F32),
                     pltpu.VMEM((TOK // 2, TOK // 2), F32)]

    seq256 = pl.BlockSpec((seq, 256), lambda b, gI: (b, gI))
    diff_units = ((0, 0, 0, 128), (1, 0, 0, 128), (0, 128, 128, 128), (1, 128, 128, 128))
    odt = pl.pallas_call(
        functools.partial(_attn_kernel, units=diff_units, diff=True, lambda_init=lambda_init),
        grid=(batch, 2),
        in_specs=[pl.BlockSpec(memory_space=pltpu.SMEM), _const_spec((DIFF_V_DIM, 1)),
                  seq256, seq256, seq256, feat_blk],
        out_specs=feat_blk,
        out_shape=ot_shape,
        scratch_shapes=[pltpu.VMEM((4, 1, TOK), F32), pltpu.VMEM((4, 1, TOK), F32),
                        pltpu.VMEM((4, DIFF_V_DIM, TOK), F32)] + score_scratch,
        compiler_params=params2,
        name="diff_attn",
    )(lam_full, diff_subln_g.reshape(DIFF_V_DIM, 1), dq0, dq1, dk, dvt)

    seq512 = pl.BlockSpec((seq, 512), lambda b, gI: (b, gI))
    mla_units = tuple((0, u * LANES, u * MLA_V, MLA_V) for u in range(4))
    omt = pl.pallas_call(
        functools.partial(_attn_kernel, units=mla_units, diff=False, lambda_init=lambda_init),
        grid=(batch, 2),
        in_specs=[seq512, seq512, feat_blk],
        out_specs=feat_blk,
        out_shape=ot_shape,
        scratch_shapes=[pltpu.VMEM((4, 1, TOK), F32), pltpu.VMEM((4, 1, TOK), F32),
                        pltpu.VMEM((4, MLA_V, TOK), F32)] + score_scratch,
        compiler_params=params2,
        name="mla_attn",
    )(qm, km, vmt)

    wod, wom, wo = w_o_diff.astype(BF16), w_o_mla.astype(BF16), w_out.astype(BF16)
    wup, wdn = w_up.astype(BF16), w_down.astype(BF16)
    row = lambda v: v.reshape(1, D_MODEL)
    out = pl.pallas_call(
        functools.partial(_out_mlp_kernel, alpha=alpha),
        grid=(n_tiles,),
        in_specs=[tok_spec(D_MODEL), feat_spec, feat_spec, tok_spec(2048),
                  _const_spec(wod.shape), _const_spec(wom.shape), _const_spec(wo.shape),
                  _const_spec((1, D_MODEL)), _const_spec((1, D_MODEL)),
                  _const_spec(wup.shape), _const_spec(wdn.shape),
                  _const_spec((1, D_MODEL)), _const_spec((1, D_MODEL))],
        out_specs=tok_spec(D_MODEL),
        out_shape=jax.ShapeDtypeStruct((n_tok, D_MODEL), F32),
        compiler_params=params1,
        name="out_mlp",
    )(x2, odt, omt, g, wod, wom, wo, row(ln1_g), row(ln1_b), wup, wdn, row(ln2_g), row(ln2_b))
    return out


def kernel(x, positions, w_in, gate_b, diff_lambda, diff_subln_g, mla_q_norm_g, w_uq, mla_kv_norm_g, w_ukv,
           w_o_diff, w_o_mla, w_out, ln1_g, ln1_b, w_up, w_down, ln2_g, ln2_b):
    batch, seq, d = x.shape
    depth = w_in.shape[0]
    alpha = (2.0 * depth) ** 0.25
    tab = _rope_freqs()
    x2 = x.reshape(batch * seq, d)
    pos2 = positions.reshape(batch * seq // PROJ_TOK, 1, PROJ_TOK)
    for l in range(depth):
        lambda_init = 0.8 - 0.6 * math.exp(-0.3 * l)
        x2 = _layer(x2, pos2, tab, w_in[l], gate_b[l], diff_lambda[l], diff_subln_g[l], mla_q_norm_g[l], w_uq[l],
                    mla_kv_norm_g[l], w_ukv[l], w_o_diff[l], w_o_mla[l], w_out[l], ln1_g[l], ln1_b[l],
                    w_up[l], w_down[l], ln2_g[l], ln2_b[l],
                    batch=batch, seq=seq, lambda_init=lambda_init, alpha=alpha)
    return x2.reshape(batch, seq, d)
```

```python
import functools
import math

import jax
import jax.numpy as jnp
import numpy as np
from jax import lax
from jax.experimental import pallas as pl
from jax.experimental.pallas import tpu as pltpu

F32 = jnp.float32
BF16 = jnp.bfloat16

D_MODEL = 1024
DIFF_HEADS = 4
DIFF_HEAD_DIM = 64
DIFF_V_DIM = 128
DIFF_ROT = 16
MLA_HEADS = 8
MLA_NOPE = 64
MLA_ROPE = 32
MLA_V = 64
MLA_Q_RANK = 384
MLA_KV_RANK = 256
D_FF = 4096
ROPE_THETA = 500000.0
LN_EPS = 1e-5
RMS_EPS = 1e-6
MASK_VALUE = -1e30

LANES = 128
TOK = 512
PROJ_TOK = 512
VMEM_LIMIT = 56 * 1024 * 1024

LOG2E = math.log2(math.e)
Q_SCALE_DIFF = (DIFF_HEAD_DIM ** -0.5) * LOG2E
Q_SCALE_MLA = ((MLA_NOPE + MLA_ROPE) ** -0.5) * LOG2E

_NT = (((1,), (1,)), ((), ()))
_TN = (((0,), (0,)), ((), ()))


def _const_spec(shape):
    n = len(shape)
    return pl.BlockSpec(shape, lambda *_: (0,) * n, pipeline_mode=pl.Buffered(1))


def _rope_freqs():
    half_d, half_m = DIFF_ROT // 2, MLA_ROPE // 2
    inv_d = jnp.power(ROPE_THETA, -jnp.arange(half_d, dtype=F32) / half_d)
    inv_m = jnp.power(ROPE_THETA, -jnp.arange(half_m, dtype=F32) / half_m)
    return jnp.concatenate([inv_d, inv_m, jnp.zeros((32 - half_d - half_m,), F32)]).reshape(32, 1)


def _rope_lane_tables(c, s):
    n = c.shape[1]
    cd, sd, cm, sm = c[0:8], s[0:8], c[8:24], s[8:24]
    ones = lambda r: jnp.ones((r, n), F32)
    zeros = lambda r: jnp.zeros((r, n), F32)
    rows = lambda *parts: jnp.concatenate(parts, axis=0).T
    cos_d = rows(cd, cd, ones(48), cd, cd, ones(48))
    lo_d = rows(-sd, zeros(56), -sd, zeros(56))
    hi_d = rows(zeros(8), sd, zeros(56), sd, zeros(48))
    cos_m = rows(ones(64), cm, cm, ones(32))
    lo_m = rows(zeros(64), -sm, zeros(48))
    hi_m = rows(zeros(80), sm, zeros(32))
    return (cos_d, lo_d, hi_d), (cos_m, lo_m, hi_m)


def _rope_slab(z, cos, sin_lo, sin_hi, half):
    up = pltpu.roll(z, LANES - half, 1)
    dn = pltpu.roll(z, half, 1)
    return z * cos + up * sin_lo + dn * sin_hi


def _in_proj_kernel(x_ref, pos_ref, tab_ref, wqt_ref, wk_ref, wvt_ref, wc_ref, wg_ref, wuqt_ref, wuk_ref, wuvt_ref,
                    gb_ref, qng_ref, kvng_ref,
                    dq0_ref, dq1_ref, dk_ref, dvt_ref, qm_ref, km_ref, vmt_ref, g_ref):
    xb = x_ref[...].astype(BF16)
    n = x_ref.shape[0]
    pos = pos_ref[0].astype(F32)
    ang = tab_ref[...] * pos
    cos_t, sin_t = jnp.cos(ang), jnp.sin(ang)
    (cos_d, lo_d, hi_d), (cos_m, lo_m, hi_m) = _rope_lane_tables(cos_t, sin_t)

    def rope_rows(x1, x2, c, s):
        return x1 * c - x2 * s, x2 * c + x1 * s

    zq = lax.dot_general(wqt_ref[...], xb, _NT, preferred_element_type=F32)
    zero_half = jnp.zeros((DIFF_HEAD_DIM, n), BF16)
    half = DIFF_ROT // 2
    for h in range(DIFF_HEADS):
        blocks = []
        for m in range(2):
            r0 = h * LANES + m * DIFF_HEAD_DIM
            o1, o2 = rope_rows(zq[r0:r0 + half], zq[r0 + half:r0 + DIFF_ROT], cos_t[0:half], sin_t[0:half])
            blk = jnp.concatenate([o1, o2, zq[r0 + DIFF_ROT:r0 + DIFF_HEAD_DIM]], axis=0) * Q_SCALE_DIFF
            blocks.append(blk.astype(BF16))
        r0 = h * LANES
        dq0_ref[0, 0, r0:r0 + DIFF_HEAD_DIM, :] = blocks[0]
        dq0_ref[0, 0, r0 + DIFF_HEAD_DIM:r0 + LANES, :] = zero_half
        dq1_ref[0, 0, r0:r0 + DIFF_HEAD_DIM, :] = zero_half
        dq1_ref[0, 0, r0 + DIFF_HEAD_DIM:r0 + LANES, :] = blocks[1]
    zk = jnp.dot(xb, wk_ref[...], preferred_element_type=F32)
    for h in range(DIFF_HEADS):
        sl = slice(h * LANES, (h + 1) * LANES)
        dk_ref[:, sl] = _rope_slab(zk[:, sl], cos_d, lo_d, hi_d, DIFF_ROT // 2).astype(BF16)
    dvt_ref[0, 0] = lax.dot_general(wvt_ref[...], xb, _NT, preferred_element_type=F32).astype(BF16)

    zc = jnp.dot(xb, wc_ref[...], preferred_element_type=F32)
    cq = zc[:, 0:MLA_Q_RANK]
    ckv = zc[:, MLA_Q_RANK:MLA_Q_RANK + MLA_KV_RANK]
    kr = zc[:, MLA_Q_RANK + MLA_KV_RANK:]
    cqn = cq * lax.rsqrt(jnp.mean(cq * cq, axis=-1, keepdims=True) + RMS_EPS) * qng_ref[...]
    ckvn = (ckv * lax.rsqrt(jnp.mean(ckv * ckv, axis=-1, keepdims=True) + RMS_EPS) * kvng_ref[...]).astype(BF16)
    qt = lax.dot_general(wuqt_ref[...], cqn.astype(BF16), _NT, preferred_element_type=F32)
    half_m = MLA_ROPE // 2
    cm, sm = cos_t[half:half + half_m], sin_t[half:half + half_m]
    for h in range(MLA_HEADS):
        r0 = h * LANES
        r1 = r0 + MLA_NOPE
        o1, o2 = rope_rows(qt[r1:r1 + half_m], qt[r1 + half_m:r1 + MLA_ROPE], cm, sm)
        blk = jnp.concatenate([qt[r0:r1], o1, o2, qt[r1 + MLA_ROPE:r0 + LANES]], axis=0) * Q_SCALE_MLA
        qm_ref[0, 0, r0:r0 + LANES, :] = blk.astype(BF16)
    kn = jnp.dot(ckvn, wuk_ref[...], preferred_element_type=F32)
    kr_rot = _rope_slab(kr, cos_m, lo_m, hi_m, MLA_ROPE // 2)
    for h in range(MLA_HEADS):
        sl = slice(h * LANES, (h + 1) * LANES)
        km_ref[:, sl] = (kn[:, sl] + kr_rot).astype(BF16)
    vmt_ref[0, 0] = lax.dot_general(wuvt_ref[...], ckvn, _NT, preferred_element_type=F32).astype(BF16)

    for c in range(4):
        sl = slice(c * 512, (c + 1) * 512)
        zg = jnp.dot(xb, wg_ref[:, sl], preferred_element_type=F32) + gb_ref[:, sl]
        g_ref[:, sl] = jax.nn.sigmoid(zg).astype(BF16)


def _attn_kernel(*refs, units, diff, lambda_init):
    if diff:
        lam_ref, gcol_ref, q0_ref, q1_ref, k_ref, vt_ref, ot_ref = refs[:7]
        q_refs = (q0_ref, q1_ref)
    else:
        q0_ref, k_ref, vt_ref, ot_ref = refs[:4]
        q_refs = (q0_ref,)
    m_sc, l_sc, acc_sc, s_sc, mb_sc, tri_sc = refs[-6:]
    n_blk = vt_ref.shape[1]
    n_units = len(units)

    key = lax.broadcasted_iota(jnp.int32, tri_sc.shape, 0)
    qry = lax.broadcasted_iota(jnp.int32, tri_sc.shape, 1)
    tri_sc[...] = jnp.where(key <= qry, 0.0, MASK_VALUE).astype(F32)

    def scores(u, qi, kj, slot, masked):
        qr, col, _, _ = units[u]
        q = q_refs[qr][0, qi, col:col + LANES, :]
        k = k_ref[pl.ds(kj * TOK, TOK), col:col + LANES]
        if not masked:
            s = jnp.dot(k, q, preferred_element_type=F32)
            s_sc[slot, u] = s
            mb_sc[slot, u] = jnp.max(s, axis=0, keepdims=True)
            return
        h = TOK // 2
        lo, hi = slice(0, h), slice(h, TOK)
        s00 = jnp.dot(k[lo], q[:, lo], preferred_element_type=F32) + tri_sc[...]
        s01 = jnp.dot(k[lo], q[:, hi], preferred_element_type=F32)
        s11 = jnp.dot(k[hi], q[:, hi], preferred_element_type=F32) + tri_sc[...]
        s_sc[slot, u, lo, lo] = s00
        s_sc[slot, u, lo, hi] = s01
        s_sc[slot, u, hi, lo] = jnp.full((h, h), MASK_VALUE, F32)
        s_sc[slot, u, hi, hi] = s11
        mb_sc[slot, u, :, lo] = jnp.max(s00, axis=0, keepdims=True)
        mb_sc[slot, u, :, hi] = jnp.maximum(jnp.max(s01, axis=0, keepdims=True), jnp.max(s11, axis=0, keepdims=True))

    def accumulate(u, kj, slot):
        _, _, vrow, vrows = units[u]
        m_old = m_sc[u]
        m_new = jnp.maximum(m_old, mb_sc[slot, u])
        alpha = jnp.exp2(m_old - m_new)
        p = jnp.exp2(s_sc[slot, u] - m_new)
        l_sc[u] = alpha * l_sc[u] + jnp.sum(p, axis=0, keepdims=True)
        m_sc[u] = m_new
        v = vt_ref[0, kj, vrow:vrow + vrows, :]
        pv = jnp.dot(v, p.astype(BF16), preferred_element_type=F32)
        acc_sc[u, 0:vrows, :] = alpha * acc_sc[u, 0:vrows, :] + pv

    def finish(u, qi):
        _, _, vrow, vrows = units[u]
        if diff:
            if u % 2 == 0:
                return
            o = acc_sc[u - 1] * (1.0 / l_sc[u - 1]) - lam_ref[0] * (acc_sc[u] * (1.0 / l_sc[u]))
            ms = jnp.mean(o * o, axis=0, keepdims=True)
            o = o * lax.rsqrt(ms + RMS_EPS) * gcol_ref[...] * (1.0 - lambda_init)
            group = (u - 1, u)
        else:
            o = acc_sc[u, 0:vrows, :] * (1.0 / l_sc[u])
            group = (u,)
        ot_ref[0, qi, vrow:vrow + vrows, :] = o.astype(BF16)
        for g in group:
            m_sc[g] = jnp.full((1, TOK), MASK_VALUE, F32)

    def step(qi, kj_next, kj_cur, cur, masked, finish_qi=None):
        for u in range(n_units):
            scores(u, qi, kj_next, 1 - cur, masked)
            accumulate(u, kj_cur, cur)
            if finish_qi is not None:
                finish(u, finish_qi)

    m_sc[...] = jnp.full(m_sc.shape, MASK_VALUE, F32)
    l_sc[...] = jnp.zeros(l_sc.shape, F32)
    acc_sc[...] = jnp.zeros(acc_sc.shape, F32)
    for u in range(n_units):
        scores(u, 0, 0, 0, True)

    def q_block_body(qi, d):
        step(qi, qi, jnp.maximum(qi - 2, 0), 1 - d, True, finish_qi=qi - 1)
        step(qi, 0, qi, d, False)

        def run(j0, n):
            for i in range(n):
                step(qi, j0 + i, j0 + i - 1, 1 - d if i % 2 == 0 else d, False)

        def k_quad(t, c):
            run(1 + 4 * t, 4)
            return c

        rest = qi - 1
        lax.fori_loop(0, rest // 4, k_quad, 0)

        @pl.when(rest % 4 >= 2)
        def _():
            run(1 + rest - rest % 4, 2)

        @pl.when(rest % 2 == 1)
        def _():
            run(qi - 1, 1)

    def q_block(qi, c):
        diag_in_slot1 = ((qi + 1) // 2) % 2 == 1

        @pl.when(diag_in_slot1)
        def _():
            q_block_body(qi, 1)

        @pl.when(jnp.logical_not(diag_in_slot1))
        def _():
            q_block_body(qi, 0)

        return c

    lax.fori_loop(1, n_blk, q_block, 0)
    last_slot = 0 if (n_blk * (n_blk + 1) // 2 - 1) % 2 == 0 else 1
    for u in range(n_units):
        accumulate(u, n_blk - 2, last_slot)
        finish(u, n_blk - 1)


def _layer_norm(r, g, b):
    mu = jnp.mean(r, axis=-1, keepdims=True)
    d = r - mu
    var = jnp.mean(d * d, axis=-1, keepdims=True)
    return d * lax.rsqrt(var + LN_EPS) * g + b


def _out_mlp_kernel(x_ref, od_ref, om_ref, g_ref, wod_ref, wom_ref, wout_ref, l1g_ref, l1b_ref,
                    wup_ref, wdn_ref, l2g_ref, l2b_ref, o_ref, *, alpha):
    half = TOK // 2
    rows = (slice(0, half), slice(half, TOK))

    def mixer_residual(r):
        yd = lax.dot_general(od_ref[0, 0, :, r], wod_ref[...], _TN, preferred_element_type=F32)
        ym = lax.dot_general(om_ref[0, 0, :, r], wom_ref[...], _TN, preferred_element_type=F32)
        y = g_ref[r, 0:D_MODEL].astype(F32) * yd + g_ref[r, D_MODEL:].astype(F32) * ym
        return alpha * x_ref[r, :] + jnp.dot(y.astype(BF16), wout_ref[...], preferred_element_type=F32)

    def mlp_hidden(x1):
        u = jnp.dot(x1.astype(BF16), wup_ref[...], preferred_element_type=F32)
        u = jnp.maximum(u, 0.0)
        return (u * u).astype(BF16)

    r_a, r_b = mixer_residual(rows[0]), mixer_residual(rows[1])
    x1_a = _layer_norm(r_a, l1g_ref[...], l1b_ref[...])
    u_a = mlp_hidden(x1_a)
    x1_b = _layer_norm(r_b, l1g_ref[...], l1b_ref[...])
    u_b = mlp_hidden(x1_b)
    f_a = jnp.dot(u_a, wdn_ref[...], preferred_element_type=F32)
    f_b = jnp.dot(u_b, wdn_ref[...], preferred_element_type=F32)
    o_ref[rows[0], :] = _layer_norm(alpha * x1_a + f_a, l2g_ref[...], l2b_ref[...])
    o_ref[rows[1], :] = _layer_norm(alpha * x1_b + f_b, l2g_ref[...], l2b_ref[...])


def _layer(x2, pos2, tab, w_in, gate_b, diff_lambda, diff_subln_g, mla_q_norm_g, w_uq, mla_kv_norm_g, w_ukv,
           w_o_diff, w_o_mla, w_out, ln1_g, ln1_b, w_up, w_down, ln2_g, ln2_b, *, batch, seq, lambda_init, alpha):
    n_tok = batch * seq
    n_blk = seq // TOK
    n_tiles = n_tok // TOK

    c0, c1, c2, c3, c4, c5 = 512, 1024, 1536, 1920, 2176, 2208
    wqt = w_in[:, 0:c0].T.astype(BF16)
    wk = w_in[:, c0:c1].astype(BF16)
    wvt = w_in[:, c1:c2].T.astype(BF16)
    kr_pad = jnp.zeros((D_MODEL, LANES), F32).at[:, MLA_NOPE:MLA_NOPE + MLA_ROPE].set(w_in[:, c4:c5])
    wc = jnp.concatenate([w_in[:, c2:c4], kr_pad], axis=1).astype(BF16)
    wg = w_in[:, c5:].astype(BF16)
    wuq3 = w_uq.reshape(MLA_Q_RANK, MLA_HEADS, MLA_NOPE + MLA_ROPE)
    wuq = jnp.pad(wuq3, ((0, 0), (0, 0), (0, LANES - MLA_NOPE - MLA_ROPE))).reshape(MLA_Q_RANK, MLA_HEADS * LANES)
    wukv3 = w_ukv.reshape(MLA_KV_RANK, MLA_HEADS, MLA_NOPE + MLA_V)
    wuk = jnp.pad(wukv3[:, :, :MLA_NOPE], ((0, 0), (0, 0), (0, LANES - MLA_NOPE))).reshape(MLA_KV_RANK, MLA_HEADS * LANES)
    wuvt = wukv3[:, :, MLA_NOPE:].reshape(MLA_KV_RANK, MLA_HEADS * MLA_V).T
    wuqt, wuk, wuvt = wuq.T.astype(BF16), wuk.astype(BF16), wuvt.astype(BF16)
    gb = gate_b.reshape(1, 2 * D_MODEL)
    lam = diff_lambda.astype(F32)
    lam_full = (jnp.exp(jnp.sum(lam[0] * lam[1])) - jnp.exp(jnp.sum(lam[2] * lam[3])) + lambda_init).reshape(1)

    tok_spec = lambda w: pl.BlockSpec((TOK, w), lambda i: (i, 0))
    feat_spec = pl.BlockSpec((1, 1, 512, TOK), lambda i: (i // n_blk, i % n_blk, 0, 0))
    params1 = pltpu.CompilerParams(dimension_semantics=("arbitrary",), vmem_limit_bytes=VMEM_LIMIT)

    sub = TOK // PROJ_TOK
    ptok_spec = lambda w: pl.BlockSpec((PROJ_TOK, w), lambda i: (i, 0))
    pfeat_spec = pl.BlockSpec((1, 1, 512, PROJ_TOK),
                              lambda i: (i // (n_blk * sub), (i // sub) % n_blk, 0, i % sub))
    pfeat_spec2 = pl.BlockSpec((1, 1, 1024, PROJ_TOK),
                               lambda i: (i // (n_blk * sub), (i // sub) % n_blk, 0, i % sub))
    feat512 = jax.ShapeDtypeStruct((batch, n_blk, 512, TOK), BF16)
    dq0, dq1, dk, dvt, qm, km, vmt, g = pl.pallas_call(
        _in_proj_kernel,
        grid=(n_tiles * sub,),
        in_specs=[ptok_spec(D_MODEL), pl.BlockSpec((1, 1, PROJ_TOK), lambda i: (i, 0, 0)), _const_spec((32, 1)),
                  _const_spec(wqt.shape), _const_spec(wk.shape), _const_spec(wvt.shape), _const_spec(wc.shape),
                  _const_spec(wg.shape), _const_spec(wuqt.shape), _const_spec(wuk.shape), _const_spec(wuvt.shape),
                  _const_spec(gb.shape), _const_spec((1, MLA_Q_RANK)), _const_spec((1, MLA_KV_RANK))],
        out_specs=[pfeat_spec, pfeat_spec, ptok_spec(512), pfeat_spec,
                   pfeat_spec2, ptok_spec(1024), pfeat_spec, ptok_spec(2048)],
        out_shape=[feat512, feat512, jax.ShapeDtypeStruct((n_tok, 512), BF16), feat512,
                   jax.ShapeDtypeStruct((batch, n_blk, 1024, TOK), BF16), jax.ShapeDtypeStruct((n_tok, 1024), BF16),
                   feat512, jax.ShapeDtypeStruct((n_tok, 2048), BF16)],
        compiler_params=params1,
        name="in_proj",
    )(x2, pos2, tab, wqt, wk, wvt, wc, wg, wuqt, wuk, wuvt, gb,
      mla_q_norm_g.reshape(1, MLA_Q_RANK), mla_kv_norm_g.reshape(1, MLA_KV_RANK))

    params2 = pltpu.CompilerParams(dimension_semantics=("arbitrary", "arbitrary"), vmem_limit_bytes=VMEM_LIMIT)
    feat_blk = pl.BlockSpec((1, n_blk, 256, TOK), lambda b, gI: (b, 0, gI, 0))
    ot_shape = jax.ShapeDtypeStruct((batch, n_blk, 512, TOK), BF16)
    score_scratch = [pltpu.VMEM((2, 4, TOK, TOK), F32), pltpu.VMEM((2, 4, 1, TOK), F32),
                     pltpu.VMEM((TOK // 2, TOK // 2), F32)]

    seq256 = pl.BlockSpec((seq, 256), lambda b, gI: (b, gI))
    diff_units = ((0, 0, 0, 128), (1, 0, 0, 128), (0, 128, 128, 128), (1, 128, 128, 128))
    odt = pl.pallas_call(
        functools.partial(_attn_kernel, units=diff_units, diff=True, lambda_init=lambda_init),
        grid=(batch, 2),
        in_specs=[pl.BlockSpec(memory_space=pltpu.SMEM), _const_spec((DIFF_V_DIM, 1)),
                  feat_blk, feat_blk, seq256, feat_blk],
        out_specs=feat_blk,
        out_shape=ot_shape,
        scratch_shapes=[pltpu.VMEM((4, 1, TOK), F32), pltpu.VMEM((4, 1, TOK), F32),
                        pltpu.VMEM((4, DIFF_V_DIM, TOK), F32)] + score_scratch,
        compiler_params=params2,
        name="diff_attn",
    )(lam_full, diff_subln_g.reshape(DIFF_V_DIM, 1), dq0, dq1, dk, dvt)

    seq512 = pl.BlockSpec((seq, 512), lambda b, gI: (b, gI))
    mla_units = tuple((0, u * LANES, u * MLA_V, MLA_V) for u in range(4))
    omt = pl.pallas_call(
        functools.partial(_attn_kernel, units=mla_units, diff=False, lambda_init=lambda_init),
        grid=(batch, 2),
        in_specs=[pl.BlockSpec((1, n_blk, 512, TOK), lambda b, gI: (b, 0, gI, 0)), seq512, feat_blk],
        out_specs=feat_blk,
        out_shape=ot_shape,
        scratch_shapes=[pltpu.VMEM((4, 1, TOK), F32), pltpu.VMEM((4, 1, TOK), F32),
                        pltpu.VMEM((4, MLA_V, TOK), F32)] + score_scratch,
        compiler_params=params2,
        name="mla_attn",
    )(qm, km, vmt)

    wod, wom, wo = w_o_diff.astype(BF16), w_o_mla.astype(BF16), w_out.astype(BF16)
    wup, wdn = w_up.astype(BF16), w_down.astype(BF16)
    row = lambda v: v.reshape(1, D_MODEL)
    out = pl.pallas_call(
        functools.partial(_out_mlp_kernel, alpha=alpha),
        grid=(n_tiles,),
        in_specs=[tok_spec(D_MODEL), feat_spec, feat_spec, tok_spec(2048),
                  _const_spec(wod.shape), _const_spec(wom.shape), _const_spec(wo.shape),
                  _const_spec((1, D_MODEL)), _const_spec((1, D_MODEL)),
                  _const_spec(wup.shape), _const_spec(wdn.shape),
                  _const_spec((1, D_MODEL)), _const_spec((1, D_MODEL))],
        out_specs=tok_spec(D_MODEL),
        out_shape=jax.ShapeDtypeStruct((n_tok, D_MODEL), F32),
        compiler_params=params1,
        name="out_mlp",
    )(x2, odt, omt, g, wod, wom, wo, row(ln1_g), row(ln1_b), wup, wdn, row(ln2_g), row(ln2_b))
    return out


def kernel(x, positions, w_in, gate_b, diff_lambda, diff_subln_g, mla_q_norm_g, w_uq, mla_kv_norm_g, w_ukv,
           w_o_diff, w_o_mla, w_out, ln1_g, ln1_b, w_up, w_down, ln2_g, ln2_b):
    batch, seq, d = x.shape
    depth = w_in.shape[0]
    alpha = (2.0 * depth) ** 0.25
    tab = _rope_freqs()
    x2 = x.reshape(batch * seq, d)
    pos2 = positions.reshape(batch * seq // PROJ_TOK, 1, PROJ_TOK)
    for l in range(depth):
        lambda_init = 0.8 - 0.6 * math.exp(-0.3 * l)
        x2 = _layer(x2, pos2, tab, w_in[l], gate_b[l], diff_lambda[l], diff_subln_g[l], mla_q_norm_g[l], w_uq[l],
                    mla_kv_norm_g[l], w_ukv[l], w_o_diff[l], w_o_mla[l], w_out[l], ln1_g[l], ln1_b[l],
                    w_up[l], w_down[l], ln2_g[l], ln2_b[l],
                    batch=batch, seq=seq, lambda_init=lambda_init, alpha=alpha)
    return x2.reshape(batch, seq, d)
```

```python
import functools
import math

import jax
import jax.numpy as jnp
import numpy as np
from jax import lax
from jax.experimental import pallas as pl
from jax.experimental.pallas import tpu as pltpu

F32 = jnp.float32
BF16 = jnp.bfloat16

D_MODEL = 1024
DIFF_HEADS = 4
DIFF_HEAD_DIM = 64
DIFF_V_DIM = 128
DIFF_ROT = 16
MLA_HEADS = 8
MLA_NOPE = 64
MLA_ROPE = 32
MLA_V = 64
MLA_QK = MLA_NOPE + MLA_ROPE
MLA_Q_RANK = 384
MLA_KV_RANK = 256
D_FF = 4096
ROPE_THETA = 500000.0
LN_EPS = 1e-5
RMS_EPS = 1e-6
MASK_VALUE = -1e30

LANES = 128
ONES_ROWS = 16
TOK = 512
PROJ_TOK = 512
VMEM_LIMIT = 56 * 1024 * 1024

LOG2E = math.log2(math.e)
Q_SCALE_DIFF = (DIFF_HEAD_DIM ** -0.5) * LOG2E
Q_SCALE_MLA = ((MLA_NOPE + MLA_ROPE) ** -0.5) * LOG2E

_NT = (((1,), (1,)), ((), ()))
_TN = (((0,), (0,)), ((), ()))


def _const_spec(shape):
    n = len(shape)
    return pl.BlockSpec(shape, lambda *_: (0,) * n, pipeline_mode=pl.Buffered(1))


def _rope_freqs():
    half_d, half_m = DIFF_ROT // 2, MLA_ROPE // 2
    inv_d = jnp.power(ROPE_THETA, -jnp.arange(half_d, dtype=F32) / half_d)
    inv_m = jnp.power(ROPE_THETA, -jnp.arange(half_m, dtype=F32) / half_m)
    return jnp.concatenate([inv_d, inv_m, jnp.zeros((32 - half_d - half_m,), F32)]).reshape(32, 1)


def _rope_lane_tables(c, s):
    n = c.shape[1]
    cd, sd, cm, sm = c[0:8], s[0:8], c[8:24], s[8:24]
    ones = lambda r: jnp.ones((r, n), F32)
    zeros = lambda r: jnp.zeros((r, n), F32)
    rows = lambda *parts: jnp.concatenate(parts, axis=0).T
    cos_d = rows(cd, cd, ones(48), cd, cd, ones(48))
    lo_d = rows(-sd, zeros(56), -sd, zeros(56))
    hi_d = rows(zeros(8), sd, zeros(56), sd, zeros(48))
    cos_m = rows(ones(64), cm, cm, ones(32))
    lo_m = rows(zeros(64), -sm, zeros(48))
    hi_m = rows(zeros(80), sm, zeros(32))
    return (cos_d, lo_d, hi_d), (cos_m, lo_m, hi_m)


def _rope_slab(z, cos, sin_lo, sin_hi, half):
    up = pltpu.roll(z, LANES - half, 1)
    dn = pltpu.roll(z, half, 1)
    return z * cos + up * sin_lo + dn * sin_hi


def _in_proj_kernel(x_ref, pos_ref, tab_ref, wqt_ref, wk_ref, wvt_ref, wc_ref, wg_ref, wuqt_ref, wuk_ref, wuvt_ref,
                    gb_ref, qng_ref, kvng_ref,
                    dq_ref, dk_ref, dvt_ref, qm_ref, km_ref, vmt_ref, g_ref):
    xb = x_ref[...].astype(BF16)
    n = x_ref.shape[0]
    pos = pos_ref[0].astype(F32)
    ang = tab_ref[...] * pos
    cos_t, sin_t = jnp.cos(ang), jnp.sin(ang)
    (cos_d, lo_d, hi_d), (cos_m, lo_m, hi_m) = _rope_lane_tables(cos_t, sin_t)

    def rope_rows(x1, x2, c, s):
        return x1 * c - x2 * s, x2 * c + x1 * s

    zq = lax.dot_general(wqt_ref[...], xb, _NT, preferred_element_type=F32)
    half = DIFF_ROT // 2
    for r0 in range(0, DIFF_HEADS * LANES, DIFF_HEAD_DIM):
        o1, o2 = rope_rows(zq[r0:r0 + half], zq[r0 + half:r0 + DIFF_ROT], cos_t[0:half], sin_t[0:half])
        blk = jnp.concatenate([o1, o2, zq[r0 + DIFF_ROT:r0 + DIFF_HEAD_DIM]], axis=0) * Q_SCALE_DIFF
        dq_ref[0, 0, r0:r0 + DIFF_HEAD_DIM, :] = blk.astype(BF16)
    zk = jnp.dot(xb, wk_ref[...], preferred_element_type=F32)
    for h in range(DIFF_HEADS):
        sl = slice(h * LANES, (h + 1) * LANES)
        dk_ref[:, sl] = _rope_slab(zk[:, sl], cos_d, lo_d, hi_d, DIFF_ROT // 2).astype(BF16)
    dvt_ref[0, 0] = lax.dot_general(wvt_ref[...], xb, _NT, preferred_element_type=F32).astype(BF16)

    for c in range(4):
        sl = slice(c * 512, (c + 1) * 512)
        zg = jnp.dot(xb, wg_ref[:, sl], preferred_element_type=F32) + gb_ref[:, sl]
        g_ref[:, sl] = jax.nn.sigmoid(zg).astype(BF16)

    zc = jnp.dot(xb, wc_ref[...], preferred_element_type=F32)
    cq = zc[:, 0:MLA_Q_RANK]
    ckv = zc[:, MLA_Q_RANK:MLA_Q_RANK + MLA_KV_RANK]
    kr = zc[:, MLA_Q_RANK + MLA_KV_RANK:]
    cqn = cq * lax.rsqrt(jnp.mean(cq * cq, axis=-1, keepdims=True) + RMS_EPS) * qng_ref[...]
    ckvn = (ckv * lax.rsqrt(jnp.mean(ckv * ckv, axis=-1, keepdims=True) + RMS_EPS) * kvng_ref[...]).astype(BF16)
    qt = lax.dot_general(wuqt_ref[...], cqn.astype(BF16), _NT, preferred_element_type=F32)
    half_m = MLA_ROPE // 2
    cm, sm = cos_t[half:half + half_m], sin_t[half:half + half_m]
    for h in range(MLA_HEADS):
        r0 = h * MLA_QK
        r1 = r0 + MLA_NOPE
        o1, o2 = rope_rows(qt[r1:r1 + half_m], qt[r1 + half_m:r1 + MLA_ROPE], cm, sm)
        blk = jnp.concatenate([qt[r0:r1], o1, o2], axis=0) * Q_SCALE_MLA
        qm_ref[0, 0, r0:r0 + MLA_QK, :] = blk.astype(BF16)
    kn = jnp.dot(ckvn, wuk_ref[...], preferred_element_type=F32)
    kr_rot = _rope_slab(kr, cos_m, lo_m, hi_m, MLA_ROPE // 2)
    for h in range(MLA_HEADS):
        sl = slice(h * LANES, (h + 1) * LANES)
        km_ref[:, sl] = (kn[:, sl] + kr_rot).astype(BF16)
    vmt_ref[0, 0] = lax.dot_general(wuvt_ref[...], ckvn, _NT, preferred_element_type=F32).astype(BF16)


def _attn_kernel(*refs, units, diff, lambda_init):
    if diff:
        lam_ref, gcol_ref, q_ref, k_ref, vt_ref, ot_ref = refs[:6]
    else:
        q_ref, k_ref, vt_ref, ot_ref = refs[:4]
    m_sc, acc_sc, s_sc, mb_sc, tri_sc = refs[-5:]
    n_blk = vt_ref.shape[1]
    n_units = len(units)

    key = lax.broadcasted_iota(jnp.int32, tri_sc.shape, 0)
    qry = lax.broadcasted_iota(jnp.int32, tri_sc.shape, 1)
    tri_sc[...] = jnp.where(key <= qry, 0.0, MASK_VALUE).astype(F32)

    def scores(u, qi, kj, slot, masked):
        qrow, qrows, zpre, col, _, _ = units[u]
        parts = [q_ref[0, qi, qrow:qrow + qrows, :]]
        if zpre:
            parts.insert(0, jnp.zeros((zpre, TOK), BF16))
        if zpre + qrows < LANES:
            parts.append(jnp.zeros((LANES - zpre - qrows, TOK), BF16))
        q = jnp.concatenate(parts, axis=0)
        k = k_ref[pl.ds(kj * TOK, TOK), col:col + LANES]
        if not masked:
            s = jnp.dot(k, q, preferred_element_type=F32)
            s_sc[slot, u] = s
            mb_sc[slot, u] = jnp.max(s, axis=0, keepdims=True)
            return
        h = TOK // 2
        lo, hi = slice(0, h), slice(h, TOK)
        s00 = jnp.dot(k[lo], q[:, lo], preferred_element_type=F32) + tri_sc[...]
        s01 = jnp.dot(k[lo], q[:, hi], preferred_element_type=F32)
        s11 = jnp.dot(k[hi], q[:, hi], preferred_element_type=F32) + tri_sc[...]
        s_sc[slot, u, lo, lo] = s00
        s_sc[slot, u, lo, hi] = s01
        s_sc[slot, u, hi, lo] = jnp.full((h, h), MASK_VALUE, F32)
        s_sc[slot, u, hi, hi] = s11
        mb_sc[slot, u, :, lo] = jnp.max(s00, axis=0, keepdims=True)
        mb_sc[slot, u, :, hi] = jnp.maximum(jnp.max(s01, axis=0, keepdims=True), jnp.max(s11, axis=0, keepdims=True))

    def accumulate(u, kj, slot):
        vrow, vrows = units[u][4:]
        m_old = m_sc[u]
        m_new = jnp.maximum(m_old, mb_sc[slot, u])
        alpha = jnp.exp2(m_old - m_new)
        p = jnp.exp2(s_sc[slot, u] - m_new)
        m_sc[u] = m_new
        v = vt_ref[0, kj, vrow:vrow + vrows, :]
        v1 = jnp.concatenate([v, jnp.ones((ONES_ROWS, TOK), BF16)], axis=0)
        pv = jnp.dot(v1, p.astype(BF16), preferred_element_type=F32)
        acc_sc[u] = alpha * acc_sc[u] + pv

    def finish(u, qi):
        vrow, vrows = units[u][4:]
        normed = lambda w: acc_sc[w, 0:vrows, :] * (1.0 / acc_sc[w, vrows:vrows + 1, :])
        if diff:
            if u % 2 == 0:
                return
            o = normed(u - 1) - lam_ref[0] * normed(u)
            ms = jnp.mean(o * o, axis=0, keepdims=True)
            o = o * lax.rsqrt(ms + RMS_EPS) * gcol_ref[...] * (1.0 - lambda_init)
            group = (u - 1, u)
        else:
            o = normed(u)
            group = (u,)
        ot_ref[0, qi, vrow:vrow + vrows, :] = o.astype(BF16)
        for g in group:
            m_sc[g] = jnp.full((1, TOK), MASK_VALUE, F32)

    def step(qi, kj_next, kj_cur, cur, masked, finish_qi=None):
        for u in range(n_units):
            scores(u, qi, kj_next, 1 - cur, masked)
            accumulate(u, kj_cur, cur)
            if finish_qi is not None:
                finish(u, finish_qi)

    m_sc[...] = jnp.full(m_sc.shape, MASK_VALUE, F32)
    acc_sc[...] = jnp.zeros(acc_sc.shape, F32)
    for u in range(n_units):
        scores(u, 0, 0, 0, True)

    def q_block_body(qi, d):
        step(qi, qi, jnp.maximum(qi - 2, 0), 1 - d, True, finish_qi=qi - 1)
        step(qi, 0, qi, d, False)

        def run(j0, n):
            for i in range(n):
                step(qi, j0 + i, j0 + i - 1, 1 - d if i % 2 == 0 else d, False)

        def k_quad(t, c):
            run(1 + 4 * t, 4)
            return c

        rest = qi - 1
        lax.fori_loop(0, rest // 4, k_quad, 0)

        @pl.when(rest % 4 >= 2)
        def _():
            run(1 + rest - rest % 4, 2)

        @pl.when(rest % 2 == 1)
        def _():
            run(qi - 1, 1)

    def q_block(qi, c):
        diag_in_slot1 = ((qi + 1) // 2) % 2 == 1

        @pl.when(diag_in_slot1)
        def _():
            q_block_body(qi, 1)

        @pl.when(jnp.logical_not(diag_in_slot1))
        def _():
            q_block_body(qi, 0)

        return c

    lax.fori_loop(1, n_blk, q_block, 0)
    last_slot = 0 if (n_blk * (n_blk + 1) // 2 - 1) % 2 == 0 else 1
    for u in range(n_units):
        accumulate(u, n_blk - 2, last_slot)
        finish(u, n_blk - 1)


def _layer_norm(r, g, b):
    mu = jnp.mean(r, axis=-1, keepdims=True)
    d = r - mu
    var = jnp.mean(d * d, axis=-1, keepdims=True)
    return d * lax.rsqrt(var + LN_EPS) * g + b


def _out_mlp_kernel(x_ref, od_ref, om_ref, g_ref, wod_ref, wom_ref, wout_ref, l1g_ref, l1b_ref,
                    wup_ref, wdn_ref, l2g_ref, l2b_ref, o_ref, *, alpha):
    half = TOK // 2
    rows = (slice(0, half), slice(half, TOK))

    def mixer_residual(r):
        yd = lax.dot_general(od_ref[0, 0, :, r], wod_ref[...], _TN, preferred_element_type=F32)
        ym = lax.dot_general(om_ref[0, 0, :, r], wom_ref[...], _TN, preferred_element_type=F32)
        y = g_ref[r, 0:D_MODEL].astype(F32) * yd + g_ref[r, D_MODEL:].astype(F32) * ym
        return alpha * x_ref[r, :] + jnp.dot(y.astype(BF16), wout_ref[...], preferred_element_type=F32)

    def mlp_hidden(x1):
        u = jnp.dot(x1.astype(BF16), wup_ref[...], preferred_element_type=F32)
        u = jnp.maximum(u, 0.0)
        return (u * u).astype(BF16)

    r_a, r_b = mixer_residual(rows[0]), mixer_residual(rows[1])
    x1_a = _layer_norm(r_a, l1g_ref[...], l1b_ref[...])
    u_a = mlp_hidden(x1_a)
    x1_b = _layer_norm(r_b, l1g_ref[...], l1b_ref[...])
    u_b = mlp_hidden(x1_b)
    f_a = jnp.dot(u_a, wdn_ref[...], preferred_element_type=F32)
    f_b = jnp.dot(u_b, wdn_ref[...], preferred_element_type=F32)
    o_ref[rows[0], :] = _layer_norm(alpha * x1_a + f_a, l2g_ref[...], l2b_ref[...])
    o_ref[rows[1], :] = _layer_norm(alpha * x1_b + f_b, l2g_ref[...], l2b_ref[...])


def _layer(x2, pos2, tab, w_in, gate_b, diff_lambda, diff_subln_g, mla_q_norm_g, w_uq, mla_kv_norm_g, w_ukv,
           w_o_diff, w_o_mla, w_out, ln1_g, ln1_b, w_up, w_down, ln2_g, ln2_b, *, batch, seq, lambda_init, alpha):
    n_tok = batch * seq
    n_blk = seq // TOK
    n_tiles = n_tok // TOK

    c0, c1, c2, c3, c4, c5 = 512, 1024, 1536, 1920, 2176, 2208
    wqt = w_in[:, 0:c0].T.astype(BF16)
    wk = w_in[:, c0:c1].astype(BF16)
    wvt = w_in[:, c1:c2].T.astype(BF16)
    kr_pad = jnp.zeros((D_MODEL, LANES), F32).at[:, MLA_NOPE:MLA_NOPE + MLA_ROPE].set(w_in[:, c4:c5])
    wc = jnp.concatenate([w_in[:, c2:c4], kr_pad], axis=1).astype(BF16)
    wg = w_in[:, c5:].astype(BF16)
    wukv3 = w_ukv.reshape(MLA_KV_RANK, MLA_HEADS, MLA_NOPE + MLA_V)
    wuk = jnp.pad(wukv3[:, :, :MLA_NOPE], ((0, 0), (0, 0), (0, LANES - MLA_NOPE))).reshape(MLA_KV_RANK, MLA_HEADS * LANES)
    wuvt = wukv3[:, :, MLA_NOPE:].reshape(MLA_KV_RANK, MLA_HEADS * MLA_V).T
    wuqt, wuk, wuvt = w_uq.T.astype(BF16), wuk.astype(BF16), wuvt.astype(BF16)
    gb = gate_b.reshape(1, 2 * D_MODEL)
    lam = diff_lambda.astype(F32)
    lam_full = (jnp.exp(jnp.sum(lam[0] * lam[1])) - jnp.exp(jnp.sum(lam[2] * lam[3])) + lambda_init).reshape(1)

    tok_spec = lambda w: pl.BlockSpec((TOK, w), lambda i: (i, 0))
    feat_spec = pl.BlockSpec((1, 1, 512, TOK), lambda i: (i // n_blk, i % n_blk, 0, 0))
    params1 = pltpu.CompilerParams(dimension_semantics=("arbitrary",), vmem_limit_bytes=VMEM_LIMIT)

    sub = TOK // PROJ_TOK
    ptok_spec = lambda w: pl.BlockSpec((PROJ_TOK, w), lambda i: (i, 0))
    pfeat_spec = pl.BlockSpec((1, 1, 512, PROJ_TOK),
                              lambda i: (i // (n_blk * sub), (i // sub) % n_blk, 0, i % sub))
    qm_rows = MLA_HEADS * MLA_QK
    pfeat_spec2 = pl.BlockSpec((1, 1, qm_rows, PROJ_TOK),
                               lambda i: (i // (n_blk * sub), (i // sub) % n_blk, 0, i % sub))
    feat512 = jax.ShapeDtypeStruct((batch, n_blk, 512, TOK), BF16)
    dq, dk, dvt, qm, km, vmt, g = pl.pallas_call(
        _in_proj_kernel,
        grid=(n_tiles * sub,),
        in_specs=[ptok_spec(D_MODEL), pl.BlockSpec((1, 1, PROJ_TOK), lambda i: (i, 0, 0)), _const_spec((32, 1)),
                  _const_spec(wqt.shape), _const_spec(wk.shape), _const_spec(wvt.shape), _const_spec(wc.shape),
                  _const_spec(wg.shape), _const_spec(wuqt.shape), _const_spec(wuk.shape), _const_spec(wuvt.shape),
                  _const_spec(gb.shape), _const_spec((1, MLA_Q_RANK)), _const_spec((1, MLA_KV_RANK))],
        out_specs=[pfeat_spec, ptok_spec(512), pfeat_spec,
                   pfeat_spec2, ptok_spec(1024), pfeat_spec, ptok_spec(2048)],
        out_shape=[feat512, jax.ShapeDtypeStruct((n_tok, 512), BF16), feat512,
                   jax.ShapeDtypeStruct((batch, n_blk, qm_rows, TOK), BF16), jax.ShapeDtypeStruct((n_tok, 1024), BF16),
                   feat512, jax.ShapeDtypeStruct((n_tok, 2048), BF16)],
        compiler_params=params1,
        name="in_proj",
    )(x2, pos2, tab, wqt, wk, wvt, wc, wg, wuqt, wuk, wuvt, gb,
      mla_q_norm_g.reshape(1, MLA_Q_RANK), mla_kv_norm_g.reshape(1, MLA_KV_RANK))

    params2 = pltpu.CompilerParams(dimension_semantics=("arbitrary", "arbitrary"), vmem_limit_bytes=VMEM_LIMIT)
    feat_blk = pl.BlockSpec((1, n_blk, 256, TOK), lambda b, gI: (b, 0, gI, 0))
    ot_shape = jax.ShapeDtypeStruct((batch, n_blk, 512, TOK), BF16)
    score_scratch = [pltpu.VMEM((2, 4, TOK, TOK), F32), pltpu.VMEM((2, 4, 1, TOK), F32),
                     pltpu.VMEM((TOK // 2, TOK // 2), F32)]

    seq256 = pl.BlockSpec((seq, 256), lambda b, gI: (b, gI))
    diff_units = tuple((hh * LANES + m * DIFF_HEAD_DIM, DIFF_HEAD_DIM, m * DIFF_HEAD_DIM, hh * LANES,
                        hh * DIFF_V_DIM, DIFF_V_DIM) for hh in range(2) for m in range(2))
    odt = pl.pallas_call(
        functools.partial(_attn_kernel, units=diff_units, diff=True, lambda_init=lambda_init),
        grid=(batch, 2),
        in_specs=[pl.BlockSpec(memory_space=pltpu.SMEM), _const_spec((DIFF_V_DIM, 1)),
                  feat_blk, seq256, feat_blk],
        out_specs=feat_blk,
        out_shape=ot_shape,
        scratch_shapes=[pltpu.VMEM((4, 1, TOK), F32),
                        pltpu.VMEM((4, DIFF_V_DIM + ONES_ROWS, TOK), F32)] + score_scratch,
        compiler_params=params2,
        name="diff_attn",
    )(lam_full, diff_subln_g.reshape(DIFF_V_DIM, 1), dq, dk, dvt)

    seq512 = pl.BlockSpec((seq, 512), lambda b, gI: (b, gI))
    mla_units = tuple((u * MLA_QK, MLA_QK, 0, u * LANES, u * MLA_V, MLA_V) for u in range(4))
    omt = pl.pallas_call(
        functools.partial(_attn_kernel, units=mla_units, diff=False, lambda_init=lambda_init),
        grid=(batch, 2),
        in_specs=[pl.BlockSpec((1, n_blk, 4 * MLA_QK, TOK), lambda b, gI: (b, 0, gI, 0)), seq512, feat_blk],
        out_specs=feat_blk,
        out_shape=ot_shape,
        scratch_shapes=[pltpu.VMEM((4, 1, TOK), F32),
                        pltpu.VMEM((4, MLA_V + ONES_ROWS, TOK), F32)] + score_scratch,
        compiler_params=params2,
        name="mla_attn",
    )(qm, km, vmt)

    wod, wom, wo = w_o_diff.astype(BF16), w_o_mla.astype(BF16), w_out.astype(BF16)
    wup, wdn = w_up.astype(BF16), w_down.astype(BF16)
    row = lambda v: v.reshape(1, D_MODEL)
    out = pl.pallas_call(
        functools.partial(_out_mlp_kernel, alpha=alpha),
        grid=(n_tiles,),
        in_specs=[tok_spec(D_MODEL), feat_spec, feat_spec, tok_spec(2048),
                  _const_spec(wod.shape), _const_spec(wom.shape), _const_spec(wo.shape),
                  _const_spec((1, D_MODEL)), _const_spec((1, D_MODEL)),
                  _const_spec(wup.shape), _const_spec(wdn.shape),
                  _const_spec((1, D_MODEL)), _const_spec((1, D_MODEL))],
        out_specs=tok_spec(D_MODEL),
        out_shape=jax.ShapeDtypeStruct((n_tok, D_MODEL), F32),
        compiler_params=params1,
        name="out_mlp",
    )(x2, odt, omt, g, wod, wom, wo, row(ln1_g), row(ln1_b), wup, wdn, row(ln2_g), row(ln2_b))
    return out


def kernel(x, positions, w_in, gate_b, diff_lambda, diff_subln_g, mla_q_norm_g, w_uq, mla_kv_norm_g, w_ukv,
           w_o_diff, w_o_mla, w_out, ln1_g, ln1_b, w_up, w_down, ln2_g, ln2_b):
    batch, seq, d = x.shape
    depth = w_in.shape[0]
    alpha = (2.0 * depth) ** 0.25
    tab = _rope_freqs()
    x2 = x.reshape(batch * seq, d)
    pos2 = positions.reshape(batch * seq // PROJ_TOK, 1, PROJ_TOK)
    for l in range(depth):
        lambda_init = 0.8 - 0.6 * math.exp(-0.3 * l)
        x2 = _layer(x2, pos2, tab, w_in[l], gate_b[l], diff_lambda[l], diff_subln_g[l], mla_q_norm_g[l], w_uq[l],
                    mla_kv_norm_g[l], w_ukv[l], w_o_diff[l], w_o_mla[l], w_out[l], ln1_g[l], ln1_b[l],
                    w_up[l], w_down[l], ln2_g[l], ln2_b[l],
                    batch=batch, seq=seq, lambda_init=lambda_init, alpha=alpha)
    return x2.reshape(batch, seq, d)
```

```python
import functools
import math

import jax
import jax.numpy as jnp
from jax import lax
from jax.experimental import pallas as pl
from jax.experimental.pallas import tpu as pltpu

F32 = jnp.float32
BF16 = jnp.bfloat16

D_MODEL = 1024
DIFF_HEADS = 4
DIFF_HEAD_DIM = 64
DIFF_V_DIM = 128
DIFF_ROT = 16
MLA_HEADS = 8
MLA_NOPE = 64
MLA_ROPE = 32
MLA_V = 64
MLA_QK = MLA_NOPE + MLA_ROPE
MLA_Q_RANK = 384
MLA_KV_RANK = 256
D_FF = 4096
ROPE_THETA = 500000.0
LN_EPS = 1e-5
RMS_EPS = 1e-6
MASK_VALUE = -1e30

LANES = 128
ONES_ROWS = 16
TOK = 512
PROJ_TOK = 512
VMEM_LIMIT = 56 * 1024 * 1024

LOG2E = math.log2(math.e)
Q_SCALE_DIFF = (DIFF_HEAD_DIM ** -0.5) * LOG2E
Q_SCALE_MLA = ((MLA_NOPE + MLA_ROPE) ** -0.5) * LOG2E

_NT = (((1,), (1,)), ((), ()))
_TN = (((0,), (0,)), ((), ()))


def _const_spec(shape):
    n = len(shape)
    return pl.BlockSpec(shape, lambda *_: (0,) * n, pipeline_mode=pl.Buffered(1))


def _rope_freqs():
    half_d, half_m = DIFF_ROT // 2, MLA_ROPE // 2
    inv_d = jnp.power(ROPE_THETA, -jnp.arange(half_d, dtype=F32) / half_d)
    inv_m = jnp.power(ROPE_THETA, -jnp.arange(half_m, dtype=F32) / half_m)
    return jnp.concatenate([inv_d, inv_m, jnp.zeros((32 - half_d - half_m,), F32)]).reshape(32, 1)


def _rope_lane_tables(c, s):
    n = c.shape[1]
    cd, sd, cm, sm = c[0:8], s[0:8], c[8:24], s[8:24]
    ones = lambda r: jnp.ones((r, n), F32)
    zeros = lambda r: jnp.zeros((r, n), F32)
    rows = lambda *parts: jnp.concatenate(parts, axis=0).T
    cos_d = rows(cd, cd, ones(48), cd, cd, ones(48))
    lo_d = rows(-sd, zeros(56), -sd, zeros(56))
    hi_d = rows(zeros(8), sd, zeros(56), sd, zeros(48))
    cos_m = rows(ones(64), cm, cm, ones(32))
    lo_m = rows(zeros(64), -sm, zeros(48))
    hi_m = rows(zeros(80), sm, zeros(32))
    return (cos_d, lo_d, hi_d), (cos_m, lo_m, hi_m)


def _rope_slab(z, cos, sin_lo, sin_hi, half):
    up = pltpu.roll(z, LANES - half, 1)
    dn = pltpu.roll(z, half, 1)
    return z * cos + up * sin_lo + dn * sin_hi


def _in_proj_kernel(x_ref, pos_ref, tab_ref, wqt_ref, wk_ref, wvt_ref, wc_ref, wg_ref, wuqt_ref, wuk_ref, wuvt_ref,
                    gb_ref, qng_ref, kvng_ref,
                    dq_ref, dk_ref, dvt_ref, qm_ref, km_ref, vmt_ref, g_ref):
    xb = x_ref[...].astype(BF16)
    pos = pos_ref[0].astype(F32)
    ang = tab_ref[...] * pos
    cos_t, sin_t = jnp.cos(ang), jnp.sin(ang)
    (cos_d, lo_d, hi_d), (cos_m, lo_m, hi_m) = _rope_lane_tables(cos_t, sin_t)

    def rope_rows(x1, x2, c, s):
        return x1 * c - x2 * s, x2 * c + x1 * s

    zq = lax.dot_general(wqt_ref[...], xb, _NT, preferred_element_type=F32)
    half = DIFF_ROT // 2
    for r0 in range(0, DIFF_HEADS * LANES, DIFF_HEAD_DIM):
        o1, o2 = rope_rows(zq[r0:r0 + half], zq[r0 + half:r0 + DIFF_ROT], cos_t[0:half], sin_t[0:half])
        blk = jnp.concatenate([o1, o2, zq[r0 + DIFF_ROT:r0 + DIFF_HEAD_DIM]], axis=0) * Q_SCALE_DIFF
        dq_ref[0, 0, r0:r0 + DIFF_HEAD_DIM, :] = blk.astype(BF16)
    zk = jnp.dot(xb, wk_ref[...], preferred_element_type=F32)
    for h in range(DIFF_HEADS):
        sl = slice(h * LANES, (h + 1) * LANES)
        dk_ref[:, sl] = _rope_slab(zk[:, sl], cos_d, lo_d, hi_d, DIFF_ROT // 2).astype(BF16)
    dvt_ref[0, 0] = lax.dot_general(wvt_ref[...], xb, _NT, preferred_element_type=F32).astype(BF16)

    zc = jnp.dot(xb, wc_ref[...], preferred_element_type=F32)
    cq = zc[:, 0:MLA_Q_RANK]
    ckv = zc[:, MLA_Q_RANK:MLA_Q_RANK + MLA_KV_RANK]
    kr = zc[:, MLA_Q_RANK + MLA_KV_RANK:]
    cqn = cq * lax.rsqrt(jnp.mean(cq * cq, axis=-1, keepdims=True) + RMS_EPS) * qng_ref[...]
    ckvn = (ckv * lax.rsqrt(jnp.mean(ckv * ckv, axis=-1, keepdims=True) + RMS_EPS) * kvng_ref[...]).astype(BF16)
    qt = lax.dot_general(wuqt_ref[...], cqn.astype(BF16), _NT, preferred_element_type=F32)
    half_m = MLA_ROPE // 2
    cm, sm = cos_t[half:half + half_m], sin_t[half:half + half_m]
    for h in range(MLA_HEADS):
        r0 = h * MLA_QK
        r1 = r0 + MLA_NOPE
        o1, o2 = rope_rows(qt[r1:r1 + half_m], qt[r1 + half_m:r1 + MLA_ROPE], cm, sm)
        blk = jnp.concatenate([qt[r0:r1], o1, o2], axis=0) * Q_SCALE_MLA
        qm_ref[0, 0, r0:r0 + MLA_QK, :] = blk.astype(BF16)
    kn = jnp.dot(ckvn, wuk_ref[...], preferred_element_type=F32)
    kr_rot = _rope_slab(kr, cos_m, lo_m, hi_m, MLA_ROPE // 2)
    for h in range(MLA_HEADS):
        sl = slice(h * LANES, (h + 1) * LANES)
        km_ref[:, sl] = (kn[:, sl] + kr_rot).astype(BF16)
    vmt_ref[0, 0] = lax.dot_general(wuvt_ref[...], ckvn, _NT, preferred_element_type=F32).astype(BF16)

    for c in range(4):
        sl = slice(c * 512, (c + 1) * 512)
        zg = jnp.dot(xb, wg_ref[:, sl], preferred_element_type=F32) + gb_ref[:, sl]
        g_ref[:, sl] = jax.nn.sigmoid(zg).astype(BF16)


def _attn_kernel(*refs, units, diff, lambda_init):
    if diff:
        lam_ref, gcol_ref, q_ref, k_ref, vt_ref, ot_ref = refs[:6]
    else:
        q_ref, k_ref, vt_ref, ot_ref = refs[:4]
    m_sc, acc_sc, s_sc, mb_sc, tri_sc = refs[-5:]
    n_blk = vt_ref.shape[1]
    n_units = len(units)

    key = lax.broadcasted_iota(jnp.int32, tri_sc.shape, 0)
    qry = lax.broadcasted_iota(jnp.int32, tri_sc.shape, 1)
    tri_sc[...] = jnp.where(key <= qry, 0.0, MASK_VALUE).astype(F32)

    def scores(u, qi, kj, slot, masked):
        qrow, qrows, zpre, col, _, _ = units[u]
        parts = [q_ref[0, qi, qrow:qrow + qrows, :]]
        if zpre:
            parts.insert(0, jnp.zeros((zpre, TOK), BF16))
        if zpre + qrows < LANES:
            parts.append(jnp.zeros((LANES - zpre - qrows, TOK), BF16))
        q = jnp.concatenate(parts, axis=0)
        k = k_ref[pl.ds(kj * TOK, TOK), col:col + LANES]
        if not masked:
            s = jnp.dot(k, q, preferred_element_type=F32)
            s_sc[slot, u] = s
            mb_sc[slot, u] = jnp.max(s, axis=0, keepdims=True)
            return
        h = TOK // 2
        lo, hi = slice(0, h), slice(h, TOK)
        s00 = jnp.dot(k[lo], q[:, lo], preferred_element_type=F32) + tri_sc[...]
        s01 = jnp.dot(k[lo], q[:, hi], preferred_element_type=F32)
        s11 = jnp.dot(k[hi], q[:, hi], preferred_element_type=F32) + tri_sc[...]
        s_sc[slot, u, lo, lo] = s00
        s_sc[slot, u, lo, hi] = s01
        s_sc[slot, u, hi, lo] = jnp.full((h, h), MASK_VALUE, F32)
        s_sc[slot, u, hi, hi] = s11
        mb_sc[slot, u, :, lo] = jnp.max(s00, axis=0, keepdims=True)
        mb_sc[slot, u, :, hi] = jnp.maximum(jnp.max(s01, axis=0, keepdims=True), jnp.max(s11, axis=0, keepdims=True))

    def accumulate(u, kj, slot):
        vrow, vrows = units[u][4:]
        m_old = m_sc[u]
        m_new = jnp.maximum(m_old, mb_sc[slot, u])
        alpha = jnp.exp2(m_old - m_new)
        p = jnp.exp2(s_sc[slot, u] - m_new)
        m_sc[u] = m_new
        v = vt_ref[0, kj, vrow:vrow + vrows, :]
        v1 = jnp.concatenate([v, jnp.ones((ONES_ROWS, TOK), BF16)], axis=0)
        pv = jnp.dot(v1, p.astype(BF16), preferred_element_type=F32)
        acc_sc[u] = alpha * acc_sc[u] + pv

    def finish(u, qi):
        vrow, vrows = units[u][4:]
        normed = lambda w: acc_sc[w, 0:vrows, :] * (1.0 / acc_sc[w, vrows:vrows + 1, :])
        if diff:
            if u % 2 == 0:
                return
            o = normed(u - 1) - lam_ref[0] * normed(u)
            ms = jnp.mean(o * o, axis=0, keepdims=True)
            o = o * lax.rsqrt(ms + RMS_EPS) * gcol_ref[...] * (1.0 - lambda_init)
            group = (u - 1, u)
        else:
            o = normed(u)
            group = (u,)
        ot_ref[0, qi, vrow:vrow + vrows, :] = o.astype(BF16)
        for g in group:
            m_sc[g] = jnp.full((1, TOK), MASK_VALUE, F32)

    def step(qi, kj_next, kj_cur, cur, masked, finish_qi=None):
        for u in range(n_units):
            scores(u, qi, kj_next, 1 - cur, masked)
            accumulate(u, kj_cur, cur)
            if finish_qi is not None:
                finish(u, finish_qi)

    m_sc[...] = jnp.full(m_sc.shape, MASK_VALUE, F32)
    acc_sc[...] = jnp.zeros(acc_sc.shape, F32)
    for u in range(n_units):
        scores(u, 0, 0, 0, True)

    def q_block_body(qi, d):
        step(qi, qi, jnp.maximum(qi - 2, 0), 1 - d, True, finish_qi=qi - 1)
        step(qi, 0, qi, d, False)

        def run(j0, n):
            for i in range(n):
                step(qi, j0 + i, j0 + i - 1, 1 - d if i % 2 == 0 else d, False)

        def k_quad(t, c):
            run(1 + 4 * t, 4)
            return c

        rest = qi - 1
        lax.fori_loop(0, rest // 4, k_quad, 0)

        @pl.when(rest % 4 >= 2)
        def _():
            run(1 + rest - rest % 4, 2)

        @pl.when(rest % 2 == 1)
        def _():
            run(qi - 1, 1)

    def q_block(qi, c):
        diag_in_slot1 = ((qi + 1) // 2) % 2 == 1

        @pl.when(diag_in_slot1)
        def _():
            q_block_body(qi, 1)

        @pl.when(jnp.logical_not(diag_in_slot1))
        def _():
            q_block_body(qi, 0)

        return c

    lax.fori_loop(1, n_blk, q_block, 0)
    last_slot = 0 if (n_blk * (n_blk + 1) // 2 - 1) % 2 == 0 else 1
    for u in range(n_units):
        accumulate(u, n_blk - 2, last_slot)
        finish(u, n_blk - 1)


def _layer_norm(r, g, b):
    mu = jnp.mean(r, axis=-1, keepdims=True)
    d = r - mu
    var = jnp.mean(d * d, axis=-1, keepdims=True)
    return d * lax.rsqrt(var + LN_EPS) * g + b


def _out_mlp_kernel(x_ref, od_ref, om_ref, g_ref, wod_ref, wom_ref, wout_ref, l1g_ref, l1b_ref,
                    wup_ref, wdn_ref, l2g_ref, l2b_ref, o_ref, *, alpha):
    half = TOK // 2
    rows = (slice(0, half), slice(half, TOK))

    def mixer_residual(r):
        yd = lax.dot_general(od_ref[0, 0, :, r], wod_ref[...], _TN, preferred_element_type=F32)
        ym = lax.dot_general(om_ref[0, 0, :, r], wom_ref[...], _TN, preferred_element_type=F32)
        y = g_ref[r, 0:D_MODEL].astype(F32) * yd + g_ref[r, D_MODEL:].astype(F32) * ym
        return alpha * x_ref[r, :] + jnp.dot(y.astype(BF16), wout_ref[...], preferred_element_type=F32)

    def mlp_hidden(x1):
        u = jnp.dot(x1.astype(BF16), wup_ref[...], preferred_element_type=F32)
        u = jnp.maximum(u, 0.0)
        return (u * u).astype(BF16)

    r_a, r_b = mixer_residual(rows[0]), mixer_residual(rows[1])
    x1_a = _layer_norm(r_a, l1g_ref[...], l1b_ref[...])
    u_a = mlp_hidden(x1_a)
    x1_b = _layer_norm(r_b, l1g_ref[...], l1b_ref[...])
    u_b = mlp_hidden(x1_b)
    f_a = jnp.dot(u_a, wdn_ref[...], preferred_element_type=F32)
    f_b = jnp.dot(u_b, wdn_ref[...], preferred_element_type=F32)
    o_ref[rows[0], :] = _layer_norm(alpha * x1_a + f_a, l2g_ref[...], l2b_ref[...])
    o_ref[rows[1], :] = _layer_norm(alpha * x1_b + f_b, l2g_ref[...], l2b_ref[...])


def _layer(x2, pos2, tab, w_in, gate_b, diff_lambda, diff_subln_g, mla_q_norm_g, w_uq, mla_kv_norm_g, w_ukv,
           w_o_diff, w_o_mla, w_out, ln1_g, ln1_b, w_up, w_down, ln2_g, ln2_b, *, batch, seq, lambda_init, alpha):
    n_tok = batch * seq
    n_blk = seq // TOK
    n_tiles = n_tok // TOK

    c0, c1, c2, c3, c4, c5 = 512, 1024, 1536, 1920, 2176, 2208
    wqt = w_in[:, 0:c0].T.astype(BF16)
    wk = w_in[:, c0:c1].astype(BF16)
    wvt = w_in[:, c1:c2].T.astype(BF16)
    kr_pad = jnp.zeros((D_MODEL, LANES), F32).at[:, MLA_NOPE:MLA_NOPE + MLA_ROPE].set(w_in[:, c4:c5])
    wc = jnp.concatenate([w_in[:, c2:c4], kr_pad], axis=1).astype(BF16)
    wg = w_in[:, c5:].astype(BF16)
    wukv3 = w_ukv.reshape(MLA_KV_RANK, MLA_HEADS, MLA_NOPE + MLA_V)
    wuk = jnp.pad(wukv3[:, :, :MLA_NOPE], ((0, 0), (0, 0), (0, LANES - MLA_NOPE))).reshape(MLA_KV_RANK, MLA_HEADS * LANES)
    wuvt = wukv3[:, :, MLA_NOPE:].reshape(MLA_KV_RANK, MLA_HEADS * MLA_V).T
    wuqt, wuk, wuvt = w_uq.T.astype(BF16), wuk.astype(BF16), wuvt.astype(BF16)
    gb = gate_b.reshape(1, 2 * D_MODEL)
    lam = diff_lambda.astype(F32)
    lam_full = (jnp.exp(jnp.sum(lam[0] * lam[1])) - jnp.exp(jnp.sum(lam[2] * lam[3])) + lambda_init).reshape(1)

    tok_spec = lambda w: pl.BlockSpec((TOK, w), lambda i: (i, 0))
    feat_spec = pl.BlockSpec((1, 1, 512, TOK), lambda i: (i // n_blk, i % n_blk, 0, 0))
    params1 = pltpu.CompilerParams(dimension_semantics=("arbitrary",), vmem_limit_bytes=VMEM_LIMIT)

    sub = TOK // PROJ_TOK
    ptok_spec = lambda w: pl.BlockSpec((PROJ_TOK, w), lambda i: (i, 0))
    pfeat_spec = pl.BlockSpec((1, 1, 512, PROJ_TOK),
                              lambda i: (i // (n_blk * sub), (i // sub) % n_blk, 0, i % sub))
    qm_rows = MLA_HEADS * MLA_QK
    pfeat_spec2 = pl.BlockSpec((1, 1, qm_rows, PROJ_TOK),
                               lambda i: (i // (n_blk * sub), (i // sub) % n_blk, 0, i % sub))
    feat512 = jax.ShapeDtypeStruct((batch, n_blk, 512, TOK), BF16)
    dq, dk, dvt, qm, km, vmt, g = pl.pallas_call(
        _in_proj_kernel,
        grid=(n_tiles * sub,),
        in_specs=[ptok_spec(D_MODEL), pl.BlockSpec((1, 1, PROJ_TOK), lambda i: (i, 0, 0)), _const_spec((32, 1)),
                  _const_spec(wqt.shape), _const_spec(wk.shape), _const_spec(wvt.shape), _const_spec(wc.shape),
                  _const_spec(wg.shape), _const_spec(wuqt.shape), _const_spec(wuk.shape), _const_spec(wuvt.shape),
                  _const_spec(gb.shape), _const_spec((1, MLA_Q_RANK)), _const_spec((1, MLA_KV_RANK))],
        out_specs=[pfeat_spec, ptok_spec(512), pfeat_spec,
                   pfeat_spec2, ptok_spec(1024), pfeat_spec, ptok_spec(2048)],
        out_shape=[feat512, jax.ShapeDtypeStruct((n_tok, 512), BF16), feat512,
                   jax.ShapeDtypeStruct((batch, n_blk, qm_rows, TOK), BF16), jax.ShapeDtypeStruct((n_tok, 1024), BF16),
                   feat512, jax.ShapeDtypeStruct((n_tok, 2048), BF16)],
        compiler_params=params1,
        name="in_proj",
    )(x2, pos2, tab, wqt, wk, wvt, wc, wg, wuqt, wuk, wuvt, gb,
      mla_q_norm_g.reshape(1, MLA_Q_RANK), mla_kv_norm_g.reshape(1, MLA_KV_RANK))

    params2 = pltpu.CompilerParams(dimension_semantics=("arbitrary", "arbitrary"), vmem_limit_bytes=VMEM_LIMIT)
    feat_blk = pl.BlockSpec((1, n_blk, 256, TOK), lambda b, gI: (b, 0, gI, 0))
    ot_shape = jax.ShapeDtypeStruct((batch, n_blk, 512, TOK), BF16)
    score_scratch = [pltpu.VMEM((2, 4, TOK, TOK), F32), pltpu.VMEM((2, 4, 1, TOK), F32),
                     pltpu.VMEM((TOK // 2, TOK // 2), F32)]

    seq256 = pl.BlockSpec((seq, 256), lambda b, gI: (b, gI))
    diff_units = tuple((hh * LANES + m * DIFF_HEAD_DIM, DIFF_HEAD_DIM, m * DIFF_HEAD_DIM, hh * LANES,
                        hh * DIFF_V_DIM, DIFF_V_DIM) for hh in range(2) for m in range(2))
    odt = pl.pallas_call(
        functools.partial(_attn_kernel, units=diff_units, diff=True, lambda_init=lambda_init),
        grid=(batch, 2),
        in_specs=[pl.BlockSpec(memory_space=pltpu.SMEM), _const_spec((DIFF_V_DIM, 1)),
                  feat_blk, seq256, feat_blk],
        out_specs=feat_blk,
        out_shape=ot_shape,
        scratch_shapes=[pltpu.VMEM((4, 1, TOK), F32),
                        pltpu.VMEM((4, DIFF_V_DIM + ONES_ROWS, TOK), F32)] + score_scratch,
        compiler_params=params2,
        name="diff_attn",
    )(lam_full, diff_subln_g.reshape(DIFF_V_DIM, 1), dq, dk, dvt)

    seq512 = pl.BlockSpec((seq, 512), lambda b, gI: (b, gI))
    mla_units = tuple((u * MLA_QK, MLA_QK, 0, u * LANES, u * MLA_V, MLA_V) for u in range(4))
    omt = pl.pallas_call(
        functools.partial(_attn_kernel, units=mla_units, diff=False, lambda_init=lambda_init),
        grid=(batch, 2),
        in_specs=[pl.BlockSpec((1, n_blk, 4 * MLA_QK, TOK), lambda b, gI: (b, 0, gI, 0)), seq512, feat_blk],
        out_specs=feat_blk,
        out_shape=ot_shape,
        scratch_shapes=[pltpu.VMEM((4, 1, TOK), F32),
                        pltpu.VMEM((4, MLA_V + ONES_ROWS, TOK), F32)] + score_scratch,
        compiler_params=params2,
        name="mla_attn",
    )(qm, km, vmt)

    wod, wom, wo = w_o_diff.astype(BF16), w_o_mla.astype(BF16), w_out.astype(BF16)
    wup, wdn = w_up.astype(BF16), w_down.astype(BF16)
    row = lambda v: v.reshape(1, D_MODEL)
    out = pl.pallas_call(
        functools.partial(_out_mlp_kernel, alpha=alpha),
        grid=(n_tiles,),
        in_specs=[tok_spec(D_MODEL), feat_spec, feat_spec, tok_spec(2048),
                  _const_spec(wod.shape), _const_spec(wom.shape), _const_spec(wo.shape),
                  _const_spec((1, D_MODEL)), _const_spec((1, D_MODEL)),
                  _const_spec(wup.shape), _const_spec(wdn.shape),
                  _const_spec((1, D_MODEL)), _const_spec((1, D_MODEL))],
        out_specs=tok_spec(D_MODEL),
        out_shape=jax.ShapeDtypeStruct((n_tok, D_MODEL), F32),
        compiler_params=params1,
        name="out_mlp",
    )(x2, odt, omt, g, wod, wom, wo, row(ln1_g), row(ln1_b), wup, wdn, row(ln2_g), row(ln2_b))
    return out


def kernel(x, positions, w_in, gate_b, diff_lambda, diff_subln_g, mla_q_norm_g, w_uq, mla_kv_norm_g, w_ukv,
           w_o_diff, w_o_mla, w_out, ln1_g, ln1_b, w_up, w_down, ln2_g, ln2_b):
    batch, seq, d = x.shape
    depth = w_in.shape[0]
    alpha = (2.0 * depth) ** 0.25
    tab = _rope_freqs()
    x2 = x.reshape(batch * seq, d)
    pos2 = positions.reshape(batch * seq // PROJ_TOK, 1, PROJ_TOK)
    for l in range(depth):
        lambda_init = 0.8 - 0.6 * math.exp(-0.3 * l)
        x2 = _layer(x2, pos2, tab, w_in[l], gate_b[l], diff_lambda[l], diff_subln_g[l], mla_q_norm_g[l], w_uq[l],
                    mla_kv_norm_g[l], w_ukv[l], w_o_diff[l], w_o_mla[l], w_out[l], ln1_g[l], ln1_b[l],
                    w_up[l], w_down[l], ln2_g[l], ln2_b[l],
                    batch=batch, seq=seq, lambda_init=lambda_init, alpha=alpha)
    return x2.reshape(batch, seq, d)
```
